```python
import math
import jax
import jax.numpy as jnp
from jax import lax
import numpy as np


D_MODEL = 1024
BATCH = 4
SEQ = 8192
DEPTH = 2

GRID_W = 64
CTX_LEN = 256
HEAD_DIM = 64
ROPE_BASE = 10000.0
NORM_EPS = 1e-6
BLOCK = 128

A_HEADS = 8
A_KV_HEADS = 2
A_GROUP = A_HEADS // A_KV_HEADS
A_WINDOW = 128
A_WIDTH = A_HEADS * HEAD_DIM
A_KV_WIDTH = A_KV_HEADS * HEAD_DIM

B_INNER = D_MODEL
B_HEAD_DIM = 64
B_HEADS = B_INNER // B_HEAD_DIM
B_GROUPS = 2
B_HPG = B_HEADS // B_GROUPS
B_STATE = 128
B_CONV = 5
B_CHUNK = 128
B_XBC = B_INNER + 2 * B_GROUPS * B_STATE

C_HEADS = D_MODEL // (2 * HEAD_DIM)
C_VDIM = 2 * HEAD_DIM
C_WIDTH = C_HEADS * C_VDIM

EVEN_SPLITS = (A_WIDTH, A_KV_WIDTH, A_KV_WIDTH, A_WIDTH, B_INNER, B_XBC, 2 * B_HEADS)
EVEN_IN = A_WIDTH + A_KV_WIDTH + A_KV_WIDTH + A_WIDTH + B_INNER + B_XBC + 2 * B_HEADS
EVEN_MIX = A_WIDTH + B_INNER
ODD_IN = 4 * C_WIDTH
N_EVEN = (DEPTH + 1) // 2
N_ODD = DEPTH // 2

kernel_name = 'hybrid_swa_ssd_diffattn_prefix_trunk'


def rms_norm(x, w):
    xf = x.astype(jnp.float32)
    y = xf * lax.rsqrt(jnp.mean(xf * xf, axis=-1, keepdims=True) + NORM_EPS)
    return (y * w.astype(jnp.float32)).astype(x.dtype)


def axial_rope(length, dtype):
    rows = length // GRID_W
    row = jnp.repeat(jnp.arange(rows), GRID_W).astype(jnp.float32)
    col = jnp.tile(jnp.arange(GRID_W), rows).astype(jnp.float32)
    n_freq = HEAD_DIM // 4
    inv_freq = ROPE_BASE ** (-jnp.arange(n_freq, dtype=jnp.float32) / n_freq)
    ang_r = row[:, None] * inv_freq
    ang_c = col[:, None] * inv_freq
    ang = jnp.concatenate([ang_r, ang_r, ang_c, ang_c], axis=-1)
    return jnp.cos(ang).astype(dtype), jnp.sin(ang).astype(dtype)


def apply_rope(x, cos, sin):
    shape = (1, x.shape[1]) + (1,) * (x.ndim - 3) + (HEAD_DIM,)
    x1, x2, x3, x4 = jnp.split(x, 4, axis=-1)
    x_rot = jnp.concatenate([-x2, x1, -x4, x3], axis=-1)
    return x * cos.reshape(shape) + x_rot * sin.reshape(shape)


def sink_softmax(parts, sink):
    m = sink
    for s in parts:
        m = jnp.maximum(m, jnp.max(s, axis=-1, keepdims=True))
    exps = [jnp.exp(s - m) for s in parts]
    denom = jnp.exp(sink - m)
    for e in exps:
        denom = denom + jnp.sum(e, axis=-1, keepdims=True)
    return [e / denom for e in exps]


def window_attention(q, k, v, ck, cv, sink):
    bsz, length = q.shape[:2]
    n_blocks = length // BLOCK
    scale = HEAD_DIM ** -0.5
    qb = q.reshape(bsz, n_blocks, BLOCK, A_KV_HEADS, A_GROUP, HEAD_DIM)

    def band(t):
        tb = jnp.pad(t.reshape(bsz, n_blocks, BLOCK, A_KV_HEADS, HEAD_DIM), ((0, 0), (1, 1), (0, 0), (0, 0), (0, 0)))
        return jnp.concatenate([tb[:, :-2], tb[:, 1:-1], tb[:, 2:]], axis=2)

    kw, vw = band(k), band(v)
    q_pos = jnp.arange(length).reshape(n_blocks, BLOCK)
    k_pos = jnp.arange(n_blocks)[:, None] * BLOCK + jnp.arange(-BLOCK, 2 * BLOCK)[None, :]
    valid = ((jnp.abs(q_pos[:, :, None] - k_pos[:, None, :]) <= A_WINDOW)
             & (k_pos[:, None, :] >= 0) & (k_pos[:, None, :] < length))
    s_loc = jnp.einsum('bnqhgd,bnkhd->bnhgqk', qb, kw).astype(jnp.float32) * scale
    s_loc = jnp.where(valid[None, :, None, None], s_loc, -jnp.inf)
    s_ctx = jnp.einsum('bnqhgd,bchd->bnhgqc', qb, ck).astype(jnp.float32) * scale
    p_loc, p_ctx = sink_softmax([s_loc, s_ctx], sink[None, None, :, :, None, None])
    out = (jnp.einsum('bnhgqk,bnkhd->bnqhgd', p_loc.astype(vw.dtype), vw)
           + jnp.einsum('bnhgqc,bchd->bnqhgd', p_ctx.astype(cv.dtype), cv))
    return out.reshape(bsz, length, A_KV_HEADS, A_GROUP, HEAD_DIM)


def context_attention(q, k, v, sink):
    s = jnp.einsum('bqhgd,bkhd->bhgqk', q, k).astype(jnp.float32) * HEAD_DIM ** -0.5
    (p,) = sink_softmax([s], sink[None, :, :, None, None])
    return jnp.einsum('bhgqk,bkhd->bqhgd', p.astype(v.dtype), v)


def depthwise_conv(u, w, b):
    out = lax.conv_general_dilated(u, w[:, None, :].astype(u.dtype), window_strides=(1,),
                                   padding=[(B_CONV // 2, B_CONV // 2)],
                                   dimension_numbers=('NWC', 'WIO', 'NWC'),
                                   feature_group_count=u.shape[-1])
    return out + b.astype(u.dtype)


def ssd_scan(x, dt, a, b, c, h0):
    bsz, length = x.shape[:2]
    nc = length // B_CHUNK
    x = x.reshape(bsz, nc, B_CHUNK, B_GROUPS, B_HPG, B_HEAD_DIM)
    dt = dt.reshape(bsz, nc, B_CHUNK, B_GROUPS, B_HPG)
    b = b.reshape(bsz, nc, B_CHUNK, B_GROUPS, B_STATE)
    c = c.reshape(bsz, nc, B_CHUNK, B_GROUPS, B_STATE)
    la = jnp.cumsum(dt * a, axis=2)
    xdt = x * dt[..., None]
    lower = jnp.tril(jnp.ones((B_CHUNK, B_CHUNK), dtype=bool))
    seg = la[:, :, :, None] - la[:, :, None]
    decay = jnp.exp(jnp.where(lower[None, None, :, :, None, None], seg, -jnp.inf))
    cb = jnp.einsum('bctgn,bcsgn->bctsg', c, b)
    y_intra = jnp.einsum('bctsg,bctsgh,bcsghp->bctghp', cb, decay, xdt)
    w_end = jnp.exp(la[:, :, -1:] - la)
    states = jnp.einsum('bcsgn,bcsgh,bcsghp->bcghpn', b, w_end, xdt)
    chunk_decay = jnp.exp(la[:, :, -1])

    def step(h, inp):
        s_c, d_c = inp
        return h * d_c[..., None, None] + s_c, h

    h_last, h_in = lax.scan(step, h0, (jnp.moveaxis(states, 1, 0), jnp.moveaxis(chunk_decay, 1, 0)))
    h_in = jnp.moveaxis(h_in, 0, 1)
    y_inter = jnp.einsum('bctgn,bcghpn,bctgh->bctghp', c, h_in, jnp.exp(la))
    return (y_intra + y_inter).reshape(bsz, length, B_GROUPS, B_HPG, B_HEAD_DIM), h_last


def flip_seq(t):
    return jnp.flip(t, axis=1)


def bidir_ssd(xs, dt, a, bm, cm, d_skip, h0_f, h0_b):
    y_f, h_f = ssd_scan(xs, dt[:, :, 0], a[0], bm, cm, h0_f)
    y_b, h_b = ssd_scan(flip_seq(xs), flip_seq(dt[:, :, 1]), a[1], flip_seq(bm), flip_seq(cm), h0_b)
    return y_f + flip_seq(y_b) + d_skip[:, :, None] * xs, h_f, h_b


def attn_ssd_layer(hx, hc, w_in, w_out, q_norm, k_norm, sink, conv_w, conv_b, dt_bias, a_log, d_skip,
                   ssm_norm, need_ctx):
    split_at = [int(s) for s in np.cumsum(EVEN_SPLITS)[:-1]]
    sink = sink.astype(jnp.float32).reshape(A_KV_HEADS, A_GROUP)
    a = -jnp.exp(a_log.astype(jnp.float32)).reshape(2, B_GROUPS, B_HPG)
    d_skip = d_skip.astype(jnp.float32).reshape(B_GROUPS, B_HPG)
    dt_bias = dt_bias.astype(jnp.float32).reshape(2, B_GROUPS, B_HPG)

    def branch_inputs(h):
        bsz, length = h.shape[:2]
        q, k, v, ga, z, xbc, dt = jnp.split(h @ w_in, split_at, axis=-1)
        q = rms_norm(q.reshape(bsz, length, A_KV_HEADS, A_GROUP, HEAD_DIM), q_norm)
        k = rms_norm(k.reshape(bsz, length, A_KV_HEADS, HEAD_DIM), k_norm)
        v = v.reshape(bsz, length, A_KV_HEADS, HEAD_DIM)
        xbc = jax.nn.silu(depthwise_conv(xbc, conv_w, conv_b)).astype(jnp.float32)
        xs, bm, cm = jnp.split(xbc, [B_INNER, B_INNER + B_GROUPS * B_STATE], axis=-1)
        xs = xs.reshape(bsz, length, B_GROUPS, B_HPG, B_HEAD_DIM)
        bm = bm.reshape(bsz, length, B_GROUPS, B_STATE)
        cm = cm.reshape(bsz, length, B_GROUPS, B_STATE)
        dt = jax.nn.softplus(dt.astype(jnp.float32).reshape(bsz, length, 2, B_GROUPS, B_HPG) + dt_bias)
        return q, k, v, ga, z, xs, dt, bm, cm

    def ssm_output(xs, dt, bm, cm, z, h0_f, h0_b):
        y, h_f, h_b = bidir_ssd(xs, dt, a, bm, cm, d_skip, h0_f, h0_b)
        y = y.reshape(z.shape).astype(z.dtype) * jax.nn.silu(z)
        return rms_norm(y, ssm_norm), h_f, h_b

    def merge(att, ga, y):
        att = att.reshape(ga.shape) * jax.nn.silu(ga)
        return jnp.concatenate([att, y], axis=-1) @ w_out

    bsz, length = hx.shape[:2]
    cq, ck, cv, cga, cz, cxs, cdt, cbm, ccm = branch_inputs(hc)
    h0 = jnp.zeros((bsz, B_GROUPS, B_HPG, B_HEAD_DIM, B_STATE), jnp.float32)
    y_c, h_f, h_b = ssm_output(cxs, cdt, cbm, ccm, cz, h0, h0)
    q, k, v, ga, z, xs, dt, bm, cm = branch_inputs(hx)
    cos, sin = axial_rope(length, hx.dtype)
    att = window_attention(apply_rope(q, cos, sin), apply_rope(k, cos, sin), v, ck, cv, sink)
    y, _, _ = ssm_output(xs, dt, bm, cm, z, h_f, h_b)
    out_x = merge(att, ga, y)
    out_c = merge(context_attention(cq, ck, cv, sink), cga, y_c) if need_ctx else None
    return out_x, out_c


def diff_attend(q, k, v, lam):
    s = jnp.einsum('bqhcd,bkhcd->bchqk', q, k).astype(jnp.float32) * HEAD_DIM ** -0.5
    p = jax.nn.softmax(s, axis=-1)
    a = (p[:, 0] - lam * p[:, 1]).astype(v.dtype)
    return jnp.einsum('bhqk,bkhv->bqhv', a, v)


def diff_attn_layer(hx, hc, w_in, w_out, q_norm, k_norm, lam_params, head_norm, lambda_init, need_ctx):
    lp = lam_params.astype(jnp.float32)
    lam = jnp.exp(jnp.sum(lp[0] * lp[1])) - jnp.exp(jnp.sum(lp[2] * lp[3])) + lambda_init

    def branch_inputs(h):
        bsz, length = h.shape[:2]
        q, k, v, g = jnp.split(h @ w_in, 4, axis=-1)
        q = rms_norm(q.reshape(bsz, length, C_HEADS, 2, HEAD_DIM), q_norm)
        k = rms_norm(k.reshape(bsz, length, C_HEADS, 2, HEAD_DIM), k_norm)
        v = v.reshape(bsz, length, C_HEADS, C_VDIM)
        return q, k, v, g

    def finish(o, g):
        o = rms_norm(o, head_norm) * (1.0 - lambda_init)
        return (o.reshape(g.shape) * jax.nn.silu(g)) @ w_out

    bsz, length = hx.shape[:2]
    cq, ck, cv, cg = branch_inputs(hc)
    q, k, v, g = branch_inputs(hx)
    cos, sin = axial_rope(length, hx.dtype)
    q = apply_rope(q, cos, sin)
    k_all = jnp.concatenate([ck, apply_rope(k, cos, sin)], axis=1)
    v_all = jnp.concatenate([cv, v], axis=1)
    n_blocks = length // BLOCK
    q_blocks = jnp.moveaxis(q.reshape(bsz, n_blocks, BLOCK, C_HEADS, 2, HEAD_DIM), 1, 0)
    o = lax.map(lambda qb: diff_attend(qb, k_all, v_all, lam), q_blocks)
    o = jnp.moveaxis(o, 0, 1).reshape(bsz, length, C_HEADS, C_VDIM)
    out_x = finish(o, g)
    out_c = finish(diff_attend(cq, ck, cv, lam), cg) if need_ctx else None
    return out_x, out_c


def setup_inputs(seed: int = 0) -> dict:
    key = jax.random.key(seed)
    ks = jax.random.split(key, 24)
    f32 = jnp.float32

    def normal(k, shape, scale):
        return jax.random.normal(k, shape, f32) * scale

    def gain(k, shape, noise=0.01):
        return 1.0 + noise * jax.random.normal(k, shape, f32)

    dt0 = jnp.exp(jax.random.uniform(ks[14], (N_EVEN, 2, B_HEADS), f32, math.log(1e-3), math.log(1e-1)))
    return {
        'x': normal(ks[0], (BATCH, SEQ, D_MODEL), 1.0),
        'c': normal(ks[1], (BATCH, D_MODEL), 1.0),
        'ctx': normal(ks[2], (BATCH, CTX_LEN, D_MODEL), 1.0),
        'c_ctx': normal(ks[3], (D_MODEL,), 1.0),
        'mod_w': normal(ks[4], (DEPTH, D_MODEL, 3 * D_MODEL), 0.5 * D_MODEL ** -0.5),
        'mod_b': normal(ks[5], (DEPTH, 3 * D_MODEL), 0.01),
        'norm_w': gain(ks[6], (DEPTH, D_MODEL)),
        'ev_w_in': normal(ks[7], (N_EVEN, D_MODEL, EVEN_IN), D_MODEL ** -0.5),
        'ev_w_out': normal(ks[8], (N_EVEN, EVEN_MIX, D_MODEL), EVEN_MIX ** -0.5),
        'ev_q_norm': gain(ks[9], (N_EVEN, HEAD_DIM)),
        'ev_k_norm': gain(ks[10], (N_EVEN, HEAD_DIM)),
        'ev_sink': normal(ks[11], (N_EVEN, A_HEADS), 0.5),
        'ev_conv_w': normal(ks[12], (N_EVEN, B_CONV, B_XBC), B_CONV ** -0.5),
        'ev_conv_b': normal(ks[13], (N_EVEN, B_XBC), 0.01),
        'ev_dt_bias': dt0 + jnp.log(-jnp.expm1(-dt0)),
        'ev_a_log': jnp.log(jax.random.uniform(ks[15], (N_EVEN, 2, B_HEADS), f32, 1.0, 16.0)),
        'ev_d_skip': gain(ks[16], (N_EVEN, B_HEADS), 0.1),
        'ev_ssm_norm': gain(ks[17], (N_EVEN, B_INNER)),
        'od_w_in': normal(ks[18], (N_ODD, D_MODEL, ODD_IN), D_MODEL ** -0.5),
        'od_w_out': normal(ks[19], (N_ODD, C_WIDTH, D_MODEL), C_WIDTH ** -0.5),
        'od_q_norm': gain(ks[20], (N_ODD, HEAD_DIM)),
        'od_k_norm': gain(ks[21], (N_ODD, HEAD_DIM)),
        'od_lambda': normal(ks[22], (N_ODD, 4, HEAD_DIM), 0.1),
        'od_head_norm': gain(ks[23], (N_ODD, C_VDIM)),
    }


def reference(x, c, ctx, c_ctx, mod_w, mod_b, norm_w, ev_w_in, ev_w_out, ev_q_norm, ev_k_norm, ev_sink,
              ev_conv_w, ev_conv_b, ev_dt_bias, ev_a_log, ev_d_skip, ev_ssm_norm, od_w_in, od_w_out,
              od_q_norm, od_k_norm, od_lambda, od_head_norm):
    for li in range(DEPTH):
        need_ctx = li < DEPTH - 1
        shift, scale, gate = jnp.split(jax.nn.silu(c) @ mod_w[li] + mod_b[li], 3, axis=-1)
        shift_c, scale_c, gate_c = jnp.split(jax.nn.silu(c_ctx) @ mod_w[li] + mod_b[li], 3, axis=-1)
        hx = rms_norm(x, norm_w[li]) * (1.0 + scale[:, None]) + shift[:, None]
        hc = rms_norm(ctx, norm_w[li]) * (1.0 + scale_c) + shift_c
        j = li // 2
        if li % 2 == 0:
            out_x, out_c = attn_ssd_layer(hx, hc, ev_w_in[j], ev_w_out[j], ev_q_norm[j], ev_k_norm[j], ev_sink[j],
                                          ev_conv_w[j], ev_conv_b[j], ev_dt_bias[j], ev_a_log[j], ev_d_skip[j],
                                          ev_ssm_norm[j], need_ctx)
        else:
            lambda_init = 0.8 - 0.6 * math.exp(-0.3 * li)
            out_x, out_c = diff_attn_layer(hx, hc, od_w_in[j], od_w_out[j], od_q_norm[j], od_k_norm[j],
                                           od_lambda[j], od_head_norm[j], lambda_init, need_ctx)
        x = x + gate[:, None] * out_x
        if need_ctx:
            ctx = ctx + gate_c * out_c
    return x
```

```python
import functools
import math

import jax
import jax.numpy as jnp
from jax import lax
from jax.experimental import pallas as pl
from jax.experimental.pallas import tpu as pltpu

D_MODEL = 1024
DEPTH = 2
GRID_W = 64
CTX_LEN = 256
HEAD_DIM = 64
ROPE_BASE = 10000.0
NORM_EPS = 1e-6
BLOCK = 128

A_HEADS = 8
A_KV_HEADS = 2
A_GROUP = A_HEADS // A_KV_HEADS
A_WIDTH = A_HEADS * HEAD_DIM
A_KV_WIDTH = A_KV_HEADS * HEAD_DIM

B_INNER = D_MODEL
B_HEAD_DIM = 64
B_HEADS = B_INNER // B_HEAD_DIM
B_GROUPS = 2
B_HPG = B_HEADS // B_GROUPS
B_STATE = 128
B_CONV = 5
B_CHUNK = 128
B_XBC = B_INNER + 2 * B_GROUPS * B_STATE

C_HEADS = D_MODEL // (2 * HEAD_DIM)
C_VDIM = 2 * HEAD_DIM
C_WIDTH = C_HEADS * C_VDIM

LANES = 128
SUBLANES = 8
ROW_TILE = 256
VMEM_LIMIT = 56 * 1024 * 1024

F32 = jnp.float32
BF16 = jnp.bfloat16
NEG_INF = float("-inf")


def _silu(x):
    return x * (1.0 / (1.0 + jnp.exp(-x)))


def _dot(a, b):
    return jnp.dot(a, b, preferred_element_type=F32)


def _dot_nt(a, b):
    return lax.dot_general(a, b, (((1,), (1,)), ((), ())), preferred_element_type=F32)


def _split2(x):
    hi = x.astype(BF16)
    lo = (x - hi.astype(F32)).astype(BF16)
    return hi, lo


def _split3(x):
    hi = x.astype(BF16)
    r = x - hi.astype(F32)
    mid = r.astype(BF16)
    lo = (r - mid.astype(F32)).astype(BF16)
    return hi, mid, lo


def _params(sem):
    return pltpu.CompilerParams(dimension_semantics=sem, vmem_limit_bytes=VMEM_LIMIT)


def _mod_kernel(cc_ref, w_ref, b_ref, o_ref):
    s = _silu(cc_ref[...])
    o_ref[...] = jnp.dot(s, w_ref[...], precision=lax.Precision.HIGHEST,
                         preferred_element_type=F32) + b_ref[...]


def _modulation(cc, mod_w, mod_b):
    depth = mod_w.shape[0]
    n_tiles = 3
    return pl.pallas_call(
        _mod_kernel,
        grid=(depth, n_tiles),
        in_specs=[
            pl.BlockSpec((SUBLANES, D_MODEL), lambda l, n: (0, 0)),
            pl.BlockSpec((None, D_MODEL, D_MODEL), lambda l, n: (l, 0, n)),
            pl.BlockSpec((None, 1, D_MODEL), lambda l, n: (l, 0, n)),
        ],
        out_specs=pl.BlockSpec((None, SUBLANES, D_MODEL), lambda l, n: (l, 0, n)),
        out_shape=jax.ShapeDtypeStruct((depth, SUBLANES, 3 * D_MODEL), F32),
        compiler_params=_params(("arbitrary", "arbitrary")),
        name="modulation",
    )(cc, mod_w, mod_b.reshape(depth, 1, 3 * D_MODEL))


def _head_norm_rope(a, nw, cos, sin_signed, seg_ones, first_quarter):
    hi, lo = _split2(a * a)
    ssq = _dot(hi, seg_ones) + _dot(lo, seg_ones)
    n = a * lax.rsqrt(ssq * (1.0 / HEAD_DIM) + NORM_EPS) * nw
    quarter = HEAD_DIM // 4
    rot = jnp.where(first_quarter, pltpu.roll(n, LANES - quarter, 1), pltpu.roll(n, quarter, 1))
    return n * cos + rot * sin_signed


def _inproj_kernel(tok_ref, mod_ref, nw_ref, w_ref, cos_ref, sin_ref, qn_ref, kn_ref, ones_ref,
                   *out_refs, groups):
    x = tok_ref[...]
    ms = jnp.mean(x * x, axis=-1, keepdims=True)
    y = x * lax.rsqrt(ms + NORM_EPS) * nw_ref[...]
    h = y * (1.0 + mod_ref[1:2, :]) + mod_ref[0:1, :]
    hb = h.astype(BF16)
    lane = lax.broadcasted_iota(jnp.int32, (x.shape[0], LANES), 1)
    first_quarter = (lane % (HEAD_DIM // 2)) < (HEAD_DIM // 4)
    for (start, width, kind), o_ref in zip(groups, out_refs):
        if kind == "plain":
            o_ref[...] = _dot(hb, w_ref[:, start:start + width]).astype(o_ref.dtype)
            continue
        nw = qn_ref[...] if kind == "q" else kn_ref[...]
        out_scale = HEAD_DIM ** -0.5 if kind == "q" else 1.0
        for s in range(width // LANES):
            a = _dot(hb, w_ref[:, start + s * LANES:start + (s + 1) * LANES])
            r = _head_norm_rope(a, nw, cos_ref[...], sin_ref[...], ones_ref[...], first_quarter)
            o_ref[:, s * LANES:(s + 1) * LANES] = (r * out_scale).astype(o_ref.dtype)


def _inproj(tok, mods, norm_w, w, cos, sin_signed, q_norm, k_norm, seg_ones, groups, out_dtypes):
    bsz, t_len, _ = tok.shape
    n_row = t_len // ROW_TILE
    row = lambda width: pl.BlockSpec((None, ROW_TILE, width), lambda b, i: (b, i, 0))
    const = lambda shape: pl.BlockSpec(shape, lambda b, i: (0,) * len(shape))
    return pl.pallas_call(
        functools.partial(_inproj_kernel, groups=groups),
        grid=(bsz, n_row),
        in_specs=[
            row(D_MODEL),
            pl.BlockSpec((None, None, SUBLANES, D_MODEL), lambda b, i: (b, jnp.minimum(i, 1), 0, 0)),
            const((1, D_MODEL)),
            const(w.shape),
            pl.BlockSpec((ROW_TILE, LANES), lambda b, i: (i, 0)),
            pl.BlockSpec((ROW_TILE, LANES), lambda b, i: (i, 0)),
            const((1, LANES)),
            const((1, LANES)),
            const((LANES, LANES)),
        ],
        out_specs=[row(width) for (_, width, _) in groups],
        out_shape=[jax.ShapeDtypeStruct((bsz, t_len, width), dt)
                   for (_, width, _), dt in zip(groups, out_dtypes)],
        compiler_params=_params(("parallel", "parallel")),
        name="inproj",
    )(tok, mods, norm_w.reshape(1, D_MODEL), w, cos, sin_signed, q_norm, k_norm, seg_ones)


def _win_attn_kernel(sink_ref, q_ref, kp_ref, kc_ref, kn_ref, kx_ref, vp_ref, vc_ref, vn_ref, vx_ref,
                     o_ref, *, n_blocks, n_ctx_blocks):
    i = pl.program_id(1)
    kk = jnp.concatenate([kp_ref[...], kc_ref[...], kn_ref[...], kx_ref[...]], axis=0)
    vv = jnp.concatenate([vp_ref[...], vc_ref[...], vn_ref[...], vx_ref[...]], axis=0)
    n_keys = kk.shape[0]
    t = lax.broadcasted_iota(jnp.int32, (BLOCK, n_keys), 0)
    c = lax.broadcasted_iota(jnp.int32, (BLOCK, n_keys), 1)
    piece = c // BLOCK
    first_key_block = jnp.where(i >= n_ctx_blocks, n_ctx_blocks, n_blocks)
    key_block = piece + (i - 1)
    offset = c - t - BLOCK
    local = ((piece < 3) & (key_block >= first_key_block) & (key_block < n_blocks)
             & (offset >= -BLOCK) & (offset <= BLOCK))
    valid = local | (piece >= 3)
    lane = lax.broadcasted_iota(jnp.int32, (BLOCK, LANES), 1)
    head_of_lane = lane // HEAD_DIM
    for g in range(A_GROUP):
        qg = q_ref[:, g * LANES:(g + 1) * LANES]
        outs = []
        for hh in range(A_KV_HEADS):
            qh = jnp.where(head_of_lane == hh, qg, jnp.zeros_like(qg))
            s = jnp.where(valid, _dot_nt(qh, kk), NEG_INF)
            sink = sink_ref[hh * A_GROUP + g]
            m = jnp.maximum(jnp.max(s, axis=-1, keepdims=True), sink)
            e = jnp.exp(s - m)
            den = jnp.exp(sink - m) + jnp.sum(e, axis=-1, keepdims=True)
            outs.append(_dot(e.astype(BF16), vv) * (1.0 / den))
        o_ref[:, g * LANES:(g + 1) * LANES] = jnp.where(head_of_lane == 0, outs[0], outs[1])


def _win_attn(sink, q, k, v):
    bsz, t_len, _ = q.shape
    n_blocks = t_len // BLOCK
    n_ctx_blocks = CTX_LEN // BLOCK
    kv = lambda fn: pl.BlockSpec((None, BLOCK, A_KV_WIDTH), fn)
    prev = lambda b, i: (b, jnp.maximum(i - 1, 0), 0)
    cur = lambda b, i: (b, i, 0)
    nxt = lambda b, i: (b, jnp.minimum(i + 1, n_blocks - 1), 0)
    ctx = pl.BlockSpec((None, CTX_LEN, A_KV_WIDTH), lambda b, i: (b, 0, 0))
    return pl.pallas_call(
        functools.partial(_win_attn_kernel, n_blocks=n_blocks, n_ctx_blocks=n_ctx_blocks),
        grid=(bsz, n_blocks),
        in_specs=[
            pl.BlockSpec(memory_space=pltpu.SMEM),
            pl.BlockSpec((None, BLOCK, A_WIDTH), cur),
            kv(prev), kv(cur), kv(nxt), ctx,
            kv(prev), kv(cur), kv(nxt), ctx,
        ],
        out_specs=pl.BlockSpec((None, BLOCK, A_WIDTH), cur),
        out_shape=jax.ShapeDtypeStruct((bsz, t_len, A_WIDTH), F32),
        compiler_params=_params(("parallel", "parallel")),
        name="window_attention",
    )(sink, q, k, k, k, k, v, v, v, v)


def _ssd_direction(d, chunk, xp_ref, xc_ref, xn_ref, dt_ref, cw_ref, cb_ref, dtb_ref, alog_ref,
                   expand_ref, state_ref, ext_ref, n_chunks, n_ctx_chunks):
    backward = d == 1
    seq_first = (chunk == 0) | (chunk == n_ctx_chunks)
    seq_last = (chunk == n_ctx_chunks - 1) | (chunk == n_chunks - 1)
    ext_ref[0:SUBLANES, :] = xp_ref[...] * jnp.where(seq_first, 0.0, 1.0)
    ext_ref[SUBLANES:SUBLANES + B_CHUNK, :] = xc_ref[...]
    ext_ref[SUBLANES + B_CHUNK:2 * SUBLANES + B_CHUNK, :] = xn_ref[...] * jnp.where(seq_last, 0.0, 1.0)
    conv = cb_ref[...]
    for j in range(B_CONV):
        off = SUBLANES - B_CONV // 2 + j
        conv = conv + ext_ref[off:off + B_CHUNK, :] * cw_ref[j:j + 1, :]
    act = _silu(conv)
    xs = act[:, :B_INNER]

    lane = lax.broadcasted_iota(jnp.int32, (B_CHUNK, LANES), 1)
    row = lax.broadcasted_iota(jnp.int32, (B_CHUNK, LANES), 0)
    dir_lanes = (lane >= d * B_HEADS) & (lane < (d + 1) * B_HEADS)
    raw = dt_ref[...] + dtb_ref[...]
    dt = jnp.maximum(raw, 0.0) + jnp.log1p(jnp.exp(-jnp.abs(raw)))
    a = -jnp.exp(alog_ref[...])
    dta = jnp.where(dir_lanes, dt * a, 0.0)
    tri_mask = (lane >= row) if backward else (lane <= row)
    tri = jnp.where(tri_mask, 1.0, 0.0).astype(BF16)
    tri_t = jnp.where((row >= lane) if backward else (row <= lane), 1.0, 0.0).astype(BF16)
    la_col = sum(_dot(tri, p) for p in _split3(dta))
    la_row = sum(_dot(p, tri_t) for p in _split3(dta.T))
    last = 0 if backward else B_CHUNK - 1
    ela = jnp.exp(la_col)
    w_end = jnp.exp(la_col[last:last + 1, :] - la_col)
    expand = expand_ref[d]

    def widen(v):
        hi, lo = _split2(jnp.where(dir_lanes, v, 0.0))
        return _dot(hi, expand) + _dot(lo, expand)

    dt_e = widen(dt)
    ela_e = widen(ela)
    wdt_e = widen(w_end * dt)
    xdt = (xs * dt_e).astype(BF16)
    wx = (xs * wdt_e).astype(BF16)
    half_of_lane = lane // B_HEAD_DIM
    slabs = []
    for g in range(B_GROUPS):
        b_g = act[:, B_INNER + g * B_STATE:B_INNER + (g + 1) * B_STATE]
        c_g = act[:, B_INNER + B_GROUPS * B_STATE + g * B_STATE:
                  B_INNER + B_GROUPS * B_STATE + (g + 1) * B_STATE].astype(BF16)
        cb = _dot_nt(c_g, b_g.astype(BF16))
        gw = B_HPG * B_HEAD_DIM
        h_in = state_ref[d, g]
        y_inter = _dot(c_g, h_in.astype(BF16)) * ela_e[:, g * gw:(g + 1) * gw]
        for pair in range(B_HPG // 2):
            col0 = g * gw + pair * LANES
            xdt_pair = xdt[:, col0:col0 + LANES]
            acc = y_inter[:, pair * LANES:(pair + 1) * LANES]
            for e in range(2):
                col = d * B_HEADS + g * B_HPG + 2 * pair + e
                seg = la_col[:, col:col + 1] - la_row[col:col + 1, :]
                decay = jnp.exp(jnp.where(tri_mask, seg, NEG_INF))
                m = (cb * decay).astype(BF16)
                rhs = jnp.where(half_of_lane == e, xdt_pair, jnp.zeros_like(xdt_pair))
                acc = acc + _dot(m, rhs)
            slabs.append(acc)
        chunk_decay = ela_e[last:last + 1, g * gw:(g + 1) * gw]
        state_ref[d, g] = h_in * chunk_decay + _dot(b_g.T.astype(BF16), wx[:, g * gw:(g + 1) * gw])
    return slabs, xs


def _ssd_kernel(xpf_ref, xcf_ref, xnf_ref, dtf_ref, xpb_ref, xcb_ref, xnb_ref, dtb_in_ref,
                cw_ref, cb_ref, dtbias_ref, alog_ref, dskip_ref, expand_ref,
                yf_ref, yb_ref, state_ref, ext_ref, *, n_chunks, n_ctx_chunks):
    j = pl.program_id(1)

    @pl.when(j == 0)
    def _():
        state_ref[...] = jnp.zeros_like(state_ref)

    chunk_f = j
    chunk_b = jnp.where(j < n_ctx_chunks, n_ctx_chunks - 1 - j, n_chunks + n_ctx_chunks - 1 - j)
    common = (cw_ref, cb_ref, dtbias_ref, alog_ref, expand_ref, state_ref, ext_ref, n_chunks, n_ctx_chunks)
    slabs, xs = _ssd_direction(0, chunk_f, xpf_ref, xcf_ref, xnf_ref, dtf_ref, *common)
    for n, slab in enumerate(slabs):
        sl = slice(n * LANES, (n + 1) * LANES)
        yf_ref[:, sl] = slab + dskip_ref[:, sl] * xs[:, sl]
    slabs, _ = _ssd_direction(1, chunk_b, xpb_ref, xcb_ref, xnb_ref, dtb_in_ref, *common)
    for n, slab in enumerate(slabs):
        yb_ref[:, n * LANES:(n + 1) * LANES] = slab


def _ssd(xbc, dt, conv_w, conv_b, dt_bias, a_log, d_skip_e, expand):
    bsz, t_len, _ = xbc.shape
    n_chunks = t_len // B_CHUNK
    n_ctx_chunks = CTX_LEN // B_CHUNK
    rows8 = B_CHUNK // SUBLANES
    n_rows8 = t_len // SUBLANES

    def chunk_b(j):
        return jnp.where(j < n_ctx_chunks, n_ctx_chunks - 1 - j, n_chunks + n_ctx_chunks - 1 - j)

    def specs(chunk_of):
        halo = lambda fn: pl.BlockSpec((None, SUBLANES, B_XBC), fn)
        return [
            halo(lambda b, j: (b, jnp.maximum(chunk_of(j) * rows8 - 1, 0), 0)),
            pl.BlockSpec((None, B_CHUNK, B_XBC), lambda b, j: (b, chunk_of(j), 0)),
            halo(lambda b, j: (b, jnp.minimum((chunk_of(j) + 1) * rows8, n_rows8 - 1), 0)),
            pl.BlockSpec((None, B_CHUNK, LANES), lambda b, j: (b, chunk_of(j), 0)),
        ]

    const = lambda shape: pl.BlockSpec(shape, lambda b, j: (0,) * len(shape))
    y_spec = lambda chunk_of: pl.BlockSpec((None, B_CHUNK, B_INNER), lambda b, j: (b, chunk_of(j), 0))
    y_shape = jax.ShapeDtypeStruct((bsz, t_len, B_INNER), F32)
    return pl.pallas_call(
        functools.partial(_ssd_kernel, n_chunks=n_chunks, n_ctx_chunks=n_ctx_chunks),
        grid=(bsz, n_chunks),
        in_specs=specs(lambda j: j) + specs(chunk_b) + [
            const((SUBLANES, B_XBC)), const((1, B_XBC)), const((1, LANES)), const((1, LANES)),
            const((1, B_INNER)), const((2, LANES, B_INNER)),
        ],
        out_specs=[y_spec(lambda j: j), y_spec(chunk_b)],
        out_shape=[y_shape, y_shape],
        scratch_shapes=[
            pltpu.VMEM((2, B_GROUPS, B_STATE, B_HPG * B_HEAD_DIM), F32),
            pltpu.VMEM((B_CHUNK + 2 * SUBLANES, B_XBC), F32),
        ],
        compiler_params=_params(("parallel", "arbitrary")),
        name="ssd_scan",
    )(xbc, xbc, xbc, dt, xbc, xbc, xbc, dt, conv_w, conv_b, dt_bias, a_log, d_skip_e, expand)


def _outproj_even_kernel(att_ref, ga_ref, yf_ref, yb_ref, z_ref, tok_ref, mod_ref, sn_ref, w_ref, o_ref):
    att = (att_ref[...] * _silu(ga_ref[...])).astype(BF16)
    y = (yf_ref[...] + yb_ref[...]) * _silu(z_ref[...])
    ms = jnp.mean(y * y, axis=-1, keepdims=True)
    yn = (y * lax.rsqrt(ms + NORM_EPS) * sn_ref[...]).astype(BF16)
    out = _dot(att, w_ref[:A_WIDTH, :]) + _dot(yn, w_ref[A_WIDTH:, :])
    o_ref[...] = tok_ref[...] + mod_ref[2:3, :] * out


def _outproj_even(att, ga, yf, yb, z, tok, mods, ssm_norm, w):
    bsz, t_len, _ = tok.shape
    row = lambda width: pl.BlockSpec((None, ROW_TILE, width), lambda b, i: (b, i, 0))
    const = lambda shape: pl.BlockSpec(shape, lambda b, i: (0,) * len(shape))
    return pl.pallas_call(
        _outproj_even_kernel,
        grid=(bsz, t_len // ROW_TILE),
        in_specs=[
            row(A_WIDTH), row(A_WIDTH), row(B_INNER), row(B_INNER), row(B_INNER), row(D_MODEL),
            pl.BlockSpec((None, None, SUBLANES, D_MODEL), lambda b, i: (b, jnp.minimum(i, 1), 0, 0)),
            const((1, B_INNER)), const(w.shape),
        ],
        out_specs=row(D_MODEL),
        out_shape=jax.ShapeDtypeStruct((bsz, t_len, D_MODEL), F32),
        compiler_params=_params(("parallel", "parallel")),
        name="outproj_even",
    )(att, ga, yf, yb, z, tok, mods, ssm_norm.reshape(1, B_INNER), w)


def _diff_attn_kernel(lam_ref, q_ref, k_ref, v_ref, hn_ref, o_ref, *, key_tile, lambda_init):
    q = q_ref[...]
    tq = q.shape[0]
    lane = lax.broadcasted_iota(jnp.int32, (tq, LANES), 1)
    zero = jnp.zeros_like(q)
    qz = [jnp.where((lane // HEAD_DIM) == c, q, zero) for c in range(2)]
    n_tiles = k_ref.shape[0] // key_tile

    def body(kt, carry):
        start = pl.multiple_of(kt * key_tile, key_tile)
        k = k_ref[pl.ds(start, key_tile), :]
        v = v_ref[pl.ds(start, key_tile), :]
        new = []
        for c in range(2):
            m, l, acc = carry[c]
            s = _dot_nt(qz[c], k)
            m_new = jnp.maximum(m, jnp.max(s, axis=-1, keepdims=True))
            alpha = jnp.exp(m - m_new)
            p = jnp.exp(s - m_new)
            l_new = alpha * l + jnp.sum(p, axis=-1, keepdims=True)
            acc_new = alpha * acc + _dot(p.astype(BF16), v)
            new.append((m_new, l_new, acc_new))
        return tuple(new)

    init = tuple((jnp.full((tq, 1), NEG_INF, F32), jnp.zeros((tq, 1), F32), jnp.zeros((tq, LANES), F32))
                 for _ in range(2))
    (_, l0, acc0), (_, l1, acc1) = lax.fori_loop(0, n_tiles, body, init)
    lp = lam_ref[...]
    lam = (jnp.exp(jnp.sum(lp[0:1] * lp[1:2], axis=-1, keepdims=True))
           - jnp.exp(jnp.sum(lp[2:3] * lp[3:4], axis=-1, keepdims=True)) + lambda_init)
    o = acc0 * (1.0 / l0) - lam * (acc1 * (1.0 / l1))
    ms = jnp.mean(o * o, axis=-1, keepdims=True)
    o_ref[...] = o * lax.rsqrt(ms + NORM_EPS) * hn_ref[...] * (1.0 - lambda_init)


def _diff_attn(lam_params, q, k, v, head_norm, lambda_init, q_tile, key_tile):
    bsz, t_len, _ = q.shape
    seq = t_len - CTX_LEN
    ctx_tiles = CTX_LEN // q_tile
    return pl.pallas_call(
        functools.partial(_diff_attn_kernel, key_tile=key_tile, lambda_init=lambda_init),
        grid=(bsz, C_HEADS, seq // q_tile),
        in_specs=[
            pl.BlockSpec(lam_params.shape, lambda b, h, i: (0, 0)),
            pl.BlockSpec((None, q_tile, C_VDIM), lambda b, h, i: (b, i + ctx_tiles, h)),
            pl.BlockSpec((None, t_len, C_VDIM), lambda b, h, i: (b, 0, h)),
            pl.BlockSpec((None, t_len, C_VDIM), lambda b, h, i: (b, 0, h)),
            pl.BlockSpec((1, C_VDIM), lambda b, h, i: (0, 0)),
        ],
        out_specs=pl.BlockSpec((None, q_tile, C_VDIM), lambda b, h, i: (b, i, h)),
        out_shape=jax.ShapeDtypeStruct((bsz, seq, C_WIDTH), F32),
        compiler_params=_params(("parallel", "parallel", "parallel")),
        name="diff_attention",
    )(lam_params, q, k, v, head_norm.reshape(1, C_VDIM))


def _outproj_odd_kernel(o_ref_in, g_ref, tok_ref, mod_ref, w_ref, o_ref):
    mix = (o_ref_in[...] * _silu(g_ref[...])).astype(BF16)
    o_ref[...] = tok_ref[...] + mod_ref[2:3, :] * _dot(mix, w_ref[...])


def _outproj_odd(o, g, tok, mods, w):
    bsz, seq, _ = o.shape
    ctx_tiles = CTX_LEN // ROW_TILE
    lat = lambda width: pl.BlockSpec((None, ROW_TILE, width), lambda b, i: (b, i, 0))
    cat = lambda width: pl.BlockSpec((None, ROW_TILE, width), lambda b, i: (b, i + ctx_tiles, 0))
    return pl.pallas_call(
        _outproj_odd_kernel,
        grid=(bsz, seq // ROW_TILE),
        in_specs=[
            lat(C_WIDTH), cat(C_WIDTH), cat(D_MODEL),
            pl.BlockSpec((None, None, SUBLANES, D_MODEL), lambda b, i: (b, 1, 0, 0)),
            pl.BlockSpec(w.shape, lambda b, i: (0, 0)),
        ],
        out_specs=lat(D_MODEL),
        out_shape=jax.ShapeDtypeStruct((bsz, seq, D_MODEL), F32),
        compiler_params=_params(("parallel", "parallel")),
        name="outproj_odd",
    )(o, g, tok, mods, w)


def _rope_tables(seq):
    rows = seq // GRID_W
    row = jnp.repeat(jnp.arange(rows), GRID_W).astype(F32)
    col = jnp.tile(jnp.arange(GRID_W), rows).astype(F32)
    n_freq = HEAD_DIM // 4
    inv_freq = ROPE_BASE ** (-jnp.arange(n_freq, dtype=F32) / n_freq)
    ang_r = row[:, None] * inv_freq
    ang_c = col[:, None] * inv_freq
    ang = jnp.concatenate([ang_r, ang_r, ang_c, ang_c], axis=-1)
    sign = jnp.tile(jnp.concatenate([-jnp.ones((n_freq,), F32), jnp.ones((n_freq,), F32)]), 2)
    cos = jnp.concatenate([jnp.ones((CTX_LEN, HEAD_DIM), F32), jnp.cos(ang)], axis=0)
    sin = jnp.concatenate([jnp.zeros((CTX_LEN, HEAD_DIM), F32), jnp.sin(ang) * sign], axis=0)
    reps = LANES // HEAD_DIM
    return jnp.tile(cos, (1, reps)), jnp.tile(sin, (1, reps))


def _pad_lanes(v):
    return jnp.pad(v, ((0, 0), (0, LANES - v.shape[1])))


def kernel(x, c, ctx, c_ctx, mod_w, mod_b, norm_w, ev_w_in, ev_w_out, ev_q_norm, ev_k_norm, ev_sink,
           ev_conv_w, ev_conv_b, ev_dt_bias, ev_a_log, ev_d_skip, ev_ssm_norm, od_w_in, od_w_out,
           od_q_norm, od_k_norm, od_lambda, od_head_norm):
    assert mod_w.shape[0] == DEPTH == 2
    bsz, seq, _ = x.shape
    assert ctx.shape[1] == CTX_LEN and seq % ROW_TILE == 0 and CTX_LEN == ROW_TILE

    cc = jnp.concatenate([c, c_ctx[None], jnp.zeros((SUBLANES - bsz - 1, D_MODEL), F32)], axis=0)
    mod_all = _modulation(cc, mod_w, mod_b)

    def mod_table(li):
        m = mod_all[li].reshape(SUBLANES, 3, D_MODEL)
        rows = jnp.stack([jnp.broadcast_to(m[bsz], (bsz, 3, D_MODEL)), m[:bsz]], axis=1)
        return jnp.pad(rows, ((0, 0), (0, 0), (0, SUBLANES - 3), (0, 0)))

    tok = jnp.concatenate([ctx, x], axis=1)
    cos, sin_signed = _rope_tables(seq)
    seg = jnp.arange(LANES) // HEAD_DIM
    seg_ones = (seg[:, None] == seg[None, :]).astype(BF16)
    tile_norm = lambda w: jnp.tile(w, LANES // HEAD_DIM).reshape(1, LANES)

    perm = jnp.array([(hh * A_GROUP + g) * HEAD_DIM + e
                      for g in range(A_GROUP) for hh in range(A_KV_HEADS) for e in range(HEAD_DIM)])
    w_in = ev_w_in[0]
    o_q, o_k, o_v, o_ga, o_z, o_xbc, o_dt = [int(s) for s in
        [0, A_WIDTH, A_WIDTH + A_KV_WIDTH, A_WIDTH + 2 * A_KV_WIDTH, 2 * A_WIDTH + 2 * A_KV_WIDTH,
         2 * A_WIDTH + 2 * A_KV_WIDTH + B_INNER, 2 * A_WIDTH + 2 * A_KV_WIDTH + B_INNER + B_XBC]]
    w0 = jnp.concatenate([
        w_in[:, o_q:o_k][:, perm], w_in[:, o_k:o_ga], w_in[:, o_ga:o_z][:, perm], w_in[:, o_z:],
        jnp.zeros((D_MODEL, LANES - 2 * B_HEADS), F32)], axis=1).astype(BF16)
    groups0 = ((o_q, A_WIDTH, "q"), (o_k, A_KV_WIDTH, "k"), (o_v, A_KV_WIDTH, "plain"),
               (o_ga, A_WIDTH, "plain"), (o_z, B_INNER, "plain"), (o_xbc, B_XBC, "plain"),
               (o_dt, LANES, "plain"))
    mods0 = mod_table(0)
    q, k, v, ga, z, xbc, dt = _inproj(
        tok, mods0, norm_w[0], w0, cos, sin_signed, tile_norm(ev_q_norm[0]), tile_norm(ev_k_norm[0]),
        seg_ones, groups0, (BF16, BF16, BF16, F32, F32, F32, F32))
    att = _win_attn(ev_sink[0], q, k, v)
    expand = jnp.stack([(jnp.arange(LANES)[:, None] == d * B_HEADS + jnp.arange(B_INNER)[None, :] // B_HEAD_DIM)
                        for d in range(2)]).astype(BF16)
    conv_w = jnp.pad(ev_conv_w[0], ((0, SUBLANES - B_CONV), (0, 0)))
    yf, yb = _ssd(xbc, dt, conv_w, ev_conv_b[0].reshape(1, B_XBC),
                  _pad_lanes(ev_dt_bias[0].reshape(1, 2 * B_HEADS)),
                  _pad_lanes(ev_a_log[0].reshape(1, 2 * B_HEADS)),
                  jnp.repeat(ev_d_skip[0], B_HEAD_DIM).reshape(1, B_INNER), expand)
    w_out0 = jnp.concatenate([ev_w_out[0][:A_WIDTH][perm], ev_w_out[0][A_WIDTH:]], axis=0).astype(BF16)
    tok = _outproj_even(att, ga, yf, yb, z, tok, mods0, ev_ssm_norm[0], w_out0)

    lambda_init = 0.8 - 0.6 * math.exp(-0.3 * 1)
    groups1 = ((0, C_WIDTH, "q"), (C_WIDTH, C_WIDTH, "k"), (2 * C_WIDTH, C_WIDTH, "plain"),
               (3 * C_WIDTH, C_WIDTH, "plain"))
    mods1 = mod_table(1)
    q, k, v, g = _inproj(
        tok, mods1, norm_w[1], od_w_in[0].astype(BF16), cos, sin_signed, tile_norm(od_q_norm[0]),
        tile_norm(od_k_norm[0]), seg_ones, groups1, (BF16, BF16, BF16, F32))
    o = _diff_attn(od_lambda[0], q, k, v, od_head_norm[0], lambda_init, q_tile=256, key_tile=768)
    return _outproj_odd(o, g, tok, mods1, od_w_out[0].astype(BF16))
```

```python
import functools
import math

import jax
import jax.numpy as jnp
from jax import lax
from jax.experimental import pallas as pl
from jax.experimental.pallas import tpu as pltpu

D_MODEL = 1024
DEPTH = 2
GRID_W = 64
CTX_LEN = 256
HEAD_DIM = 64
ROPE_BASE = 10000.0
NORM_EPS = 1e-6
BLOCK = 128

A_HEADS = 8
A_KV_HEADS = 2
A_GROUP = A_HEADS // A_KV_HEADS
A_WIDTH = A_HEADS * HEAD_DIM
A_KV_WIDTH = A_KV_HEADS * HEAD_DIM

B_INNER = D_MODEL
B_HEAD_DIM = 64
B_HEADS = B_INNER // B_HEAD_DIM
B_GROUPS = 2
B_HPG = B_HEADS // B_GROUPS
B_STATE = 128
B_CONV = 5
B_CHUNK = 128
B_XBC = B_INNER + 2 * B_GROUPS * B_STATE

C_HEADS = D_MODEL // (2 * HEAD_DIM)
C_VDIM = 2 * HEAD_DIM
C_WIDTH = C_HEADS * C_VDIM

LANES = 128
SUBLANES = 8
ROW_TILE = 256
ATTN_Q_TILE = 512
ATTN_KEY_TILE = 768
VMEM_LIMIT = 56 * 1024 * 1024

F32 = jnp.float32
BF16 = jnp.bfloat16
NEG_INF = float("-inf")
LOG2E = math.log2(math.e)


def _silu(x):
    return x * (1.0 / (1.0 + jnp.exp(-x)))


def _dot(a, b):
    return jnp.dot(a, b, preferred_element_type=F32)


def _dot_nt(a, b):
    return lax.dot_general(a, b, (((1,), (1,)), ((), ())), preferred_element_type=F32)


def _split2(x):
    hi = x.astype(BF16)
    lo = (x - hi.astype(F32)).astype(BF16)
    return hi, lo


def _split3(x):
    hi = x.astype(BF16)
    r = x - hi.astype(F32)
    mid = r.astype(BF16)
    lo = (r - mid.astype(F32)).astype(BF16)
    return hi, mid, lo


def _params(sem):
    return pltpu.CompilerParams(dimension_semantics=sem, vmem_limit_bytes=VMEM_LIMIT)


def _mod_kernel(cc_ref, w_ref, b_ref, o_ref):
    s = _silu(cc_ref[...])
    o_ref[...] = jnp.dot(s, w_ref[...], precision=lax.Precision.HIGHEST,
                         preferred_element_type=F32) + b_ref[...]


def _modulation(cc, mod_w, mod_b):
    depth = mod_w.shape[0]
    n_tiles = 3
    return pl.pallas_call(
        _mod_kernel,
        grid=(depth, n_tiles),
        in_specs=[
            pl.BlockSpec((SUBLANES, D_MODEL), lambda l, n: (0, 0)),
            pl.BlockSpec((None, D_MODEL, D_MODEL), lambda l, n: (l, 0, n)),
            pl.BlockSpec((None, 1, D_MODEL), lambda l, n: (l, 0, n)),
        ],
        out_specs=pl.BlockSpec((None, SUBLANES, D_MODEL), lambda l, n: (l, 0, n)),
        out_shape=jax.ShapeDtypeStruct((depth, SUBLANES, 3 * D_MODEL), F32),
        compiler_params=_params(("arbitrary", "arbitrary")),
        name="modulation",
    )(cc, mod_w, mod_b.reshape(depth, 1, 3 * D_MODEL))


def _head_norm_rope(a, nw, cos, sin_signed, seg_ones, first_quarter):
    hi, lo = _split2(a * a)
    ssq = _dot(hi, seg_ones) + _dot(lo, seg_ones)
    n = a * lax.rsqrt(ssq * (1.0 / HEAD_DIM) + NORM_EPS) * nw
    quarter = HEAD_DIM // 4
    rot = jnp.where(first_quarter, pltpu.roll(n, LANES - quarter, 1), pltpu.roll(n, quarter, 1))
    return n * cos + rot * sin_signed


def _inproj_kernel(tok_ref, mod_ref, nw_ref, w_ref, cos_ref, sin_ref, qn_ref, kn_ref, ones_ref,
                   *out_refs, groups):
    x = tok_ref[...]
    ms = jnp.mean(x * x, axis=-1, keepdims=True)
    y = x * lax.rsqrt(ms + NORM_EPS) * nw_ref[...]
    h = y * (1.0 + mod_ref[1:2, :]) + mod_ref[0:1, :]
    hb = h.astype(BF16)
    lane = lax.broadcasted_iota(jnp.int32, (x.shape[0], LANES), 1)
    first_quarter = (lane % (HEAD_DIM // 2)) < (HEAD_DIM // 4)
    for (start, width, kind, out_scale, _), o_ref in zip(groups, out_refs):
        if kind == "plain":
            o_ref[...] = _dot(hb, w_ref[:, start:start + width]).astype(o_ref.dtype)
            continue
        nw = qn_ref[...] if kind == "q" else kn_ref[...]
        for s in range(width // LANES):
            a = _dot(hb, w_ref[:, start + s * LANES:start + (s + 1) * LANES])
            r = _head_norm_rope(a, nw, cos_ref[...], sin_ref[...], ones_ref[...], first_quarter)
            o_ref[:, s * LANES:(s + 1) * LANES] = (r * out_scale).astype(o_ref.dtype)


def _inproj(tok, mods, norm_w, w, cos, sin_signed, q_norm, k_norm, seg_ones, groups, out_dtypes):
    bsz, t_len, _ = tok.shape
    n_row = t_len // ROW_TILE
    ctx_tiles = CTX_LEN // ROW_TILE
    row = lambda width: pl.BlockSpec((None, ROW_TILE, width), lambda b, i: (b, i, 0))
    lat = lambda width: pl.BlockSpec((None, ROW_TILE, width), lambda b, i: (b, jnp.maximum(i - ctx_tiles, 0), 0))
    const = lambda shape: pl.BlockSpec(shape, lambda b, i: (0,) * len(shape))
    return pl.pallas_call(
        functools.partial(_inproj_kernel, groups=groups),
        grid=(bsz, n_row),
        in_specs=[
            row(D_MODEL),
            pl.BlockSpec((None, None, SUBLANES, D_MODEL), lambda b, i: (b, jnp.minimum(i, 1), 0, 0)),
            const((1, D_MODEL)),
            const(w.shape),
            pl.BlockSpec((ROW_TILE, LANES), lambda b, i: (i, 0)),
            pl.BlockSpec((ROW_TILE, LANES), lambda b, i: (i, 0)),
            const((1, LANES)),
            const((1, LANES)),
            const((LANES, LANES)),
        ],
        out_specs=[(lat if latent_only else row)(width) for (_, width, _, _, latent_only) in groups],
        out_shape=[jax.ShapeDtypeStruct((bsz, t_len - CTX_LEN if latent_only else t_len, width), dt)
                   for (_, width, _, _, latent_only), dt in zip(groups, out_dtypes)],
        compiler_params=_params(("parallel", "arbitrary")),
        name="inproj",
    )(tok, mods, norm_w.reshape(1, D_MODEL), w, cos, sin_signed, q_norm, k_norm, seg_ones)


def _win_attn_kernel(sink_ref, q_ref, kp_ref, kc_ref, kn_ref, kx_ref, vp_ref, vc_ref, vn_ref, vx_ref,
                     o_ref, *, n_blocks, n_ctx_blocks):
    i = pl.program_id(1)
    kk = jnp.concatenate([kp_ref[...], kc_ref[...], kn_ref[...], kx_ref[...]], axis=0)
    vv = jnp.concatenate([vp_ref[...], vc_ref[...], vn_ref[...], vx_ref[...]], axis=0)
    n_keys = kk.shape[0]
    t = lax.broadcasted_iota(jnp.int32, (BLOCK, n_keys), 0)
    c = lax.broadcasted_iota(jnp.int32, (BLOCK, n_keys), 1)
    piece = c // BLOCK
    first_key_block = jnp.where(i >= n_ctx_blocks, n_ctx_blocks, n_blocks)
    key_block = piece + (i - 1)
    offset = c - t - BLOCK
    local = ((piece < 3) & (key_block >= first_key_block) & (key_block < n_blocks)
             & (offset >= -BLOCK) & (offset <= BLOCK))
    valid = local | (piece >= 3)
    lane = lax.broadcasted_iota(jnp.int32, (BLOCK, LANES), 1)
    head_of_lane = lane // HEAD_DIM
    for g in range(A_GROUP):
        qg = q_ref[:, g * LANES:(g + 1) * LANES]
        outs = []
        for hh in range(A_KV_HEADS):
            qh = jnp.where(head_of_lane == hh, qg, jnp.zeros_like(qg))
            s = jnp.where(valid, _dot_nt(qh, kk), NEG_INF)
            sink = sink_ref[hh * A_GROUP + g]
            m = jnp.maximum(jnp.max(s, axis=-1, keepdims=True), sink)
            e = jnp.exp(s - m)
            den = jnp.exp(sink - m) + jnp.sum(e, axis=-1, keepdims=True)
            outs.append(_dot(e.astype(BF16), vv) * (1.0 / den))
        o_ref[:, g * LANES:(g + 1) * LANES] = jnp.where(head_of_lane == 0, outs[0], outs[1])


def _win_attn(sink, q, k, v):
    bsz, t_len, _ = q.shape
    n_blocks = t_len // BLOCK
    n_ctx_blocks = CTX_LEN // BLOCK
    kv = lambda fn: pl.BlockSpec((None, BLOCK, A_KV_WIDTH), fn)
    prev = lambda b, i: (b, jnp.maximum(i - 1, 0), 0)
    cur = lambda b, i: (b, i, 0)
    nxt = lambda b, i: (b, jnp.minimum(i + 1, n_blocks - 1), 0)
    ctx = pl.BlockSpec((None, CTX_LEN, A_KV_WIDTH), lambda b, i: (b, 0, 0))
    return pl.pallas_call(
        functools.partial(_win_attn_kernel, n_blocks=n_blocks, n_ctx_blocks=n_ctx_blocks),
        grid=(bsz, n_blocks),
        in_specs=[
            pl.BlockSpec(memory_space=pltpu.SMEM),
            pl.BlockSpec((None, BLOCK, A_WIDTH), cur),
            kv(prev), kv(cur), kv(nxt), ctx,
            kv(prev), kv(cur), kv(nxt), ctx,
        ],
        out_specs=pl.BlockSpec((None, BLOCK, A_WIDTH), cur),
        out_shape=jax.ShapeDtypeStruct((bsz, t_len, A_WIDTH), F32),
        compiler_params=_params(("parallel", "parallel")),
        name="window_attention",
    )(sink, q, k, k, k, k, v, v, v, v)


def _ssd_direction(d, chunk, xp_ref, xc_ref, xn_ref, dt_ref, cw_ref, cb_ref, dtb_ref, alog_ref,
                   expand_ref, state_ref, ext_ref, n_chunks, n_ctx_chunks):
    backward = d == 1
    seq_first = (chunk == 0) | (chunk == n_ctx_chunks)
    seq_last = (chunk == n_ctx_chunks - 1) | (chunk == n_chunks - 1)
    ext_ref[0:SUBLANES, :] = xp_ref[...] * jnp.where(seq_first, 0.0, 1.0)
    ext_ref[SUBLANES:SUBLANES + B_CHUNK, :] = xc_ref[...]
    ext_ref[SUBLANES + B_CHUNK:2 * SUBLANES + B_CHUNK, :] = xn_ref[...] * jnp.where(seq_last, 0.0, 1.0)
    conv = cb_ref[...]
    for j in range(B_CONV):
        off = SUBLANES - B_CONV // 2 + j
        conv = conv + ext_ref[off:off + B_CHUNK, :] * cw_ref[j:j + 1, :]
    act = _silu(conv)
    xs = act[:, :B_INNER]

    lane = lax.broadcasted_iota(jnp.int32, (B_CHUNK, LANES), 1)
    row = lax.broadcasted_iota(jnp.int32, (B_CHUNK, LANES), 0)
    dir_lanes = (lane >= d * B_HEADS) & (lane < (d + 1) * B_HEADS)
    raw = dt_ref[...] + dtb_ref[...]
    dt = jnp.maximum(raw, 0.0) + jnp.log1p(jnp.exp(-jnp.abs(raw)))
    a = -jnp.exp(alog_ref[...])
    dta = jnp.where(dir_lanes, dt * a, 0.0)
    tri_mask = (lane >= row) if backward else (lane <= row)
    tri = jnp.where(tri_mask, 1.0, 0.0).astype(BF16)
    tri_t = jnp.where((row >= lane) if backward else (row <= lane), 1.0, 0.0).astype(BF16)
    la_col = sum(_dot(tri, p) for p in _split3(dta))
    la_row = sum(_dot(p, tri_t) for p in _split3(dta.T))
    last = 0 if backward else B_CHUNK - 1
    ela = jnp.exp(la_col)
    w_end = jnp.exp(la_col[last:last + 1, :] - la_col)
    expand = expand_ref[d]

    def widen(v):
        hi, lo = _split2(jnp.where(dir_lanes, v, 0.0))
        return _dot(hi, expand) + _dot(lo, expand)

    dt_e = widen(dt)
    ela_e = widen(ela)
    wdt_e = widen(w_end * dt)
    xdt = (xs * dt_e).astype(BF16)
    wx = (xs * wdt_e).astype(BF16)
    half_of_lane = lane // B_HEAD_DIM
    slabs = []
    for g in range(B_GROUPS):
        b_g = act[:, B_INNER + g * B_STATE:B_INNER + (g + 1) * B_STATE]
        c_g = act[:, B_INNER + B_GROUPS * B_STATE + g * B_STATE:
                  B_INNER + B_GROUPS * B_STATE + (g + 1) * B_STATE].astype(BF16)
        cb = _dot_nt(c_g, b_g.astype(BF16))
        gw = B_HPG * B_HEAD_DIM
        h_in = state_ref[d, g]
        y_inter = _dot(c_g, h_in.astype(BF16)) * ela_e[:, g * gw:(g + 1) * gw]
        for pair in range(B_HPG // 2):
            col0 = g * gw + pair * LANES
            xdt_pair = xdt[:, col0:col0 + LANES]
            acc = y_inter[:, pair * LANES:(pair + 1) * LANES]
            for e in range(2):
                col = d * B_HEADS + g * B_HPG + 2 * pair + e
                seg = la_col[:, col:col + 1] - la_row[col:col + 1, :]
                decay = jnp.exp(jnp.where(tri_mask, seg, NEG_INF))
                m = (cb * decay).astype(BF16)
                rhs = jnp.where(half_of_lane == e, xdt_pair, jnp.zeros_like(xdt_pair))
                acc = acc + _dot(m, rhs)
            slabs.append(acc)
        chunk_decay = ela_e[last:last + 1, g * gw:(g + 1) * gw]
        state_ref[d, g] = h_in * chunk_decay + _dot(b_g.T.astype(BF16), wx[:, g * gw:(g + 1) * gw])
    return slabs, xs


def _ssd_kernel(xpf_ref, xcf_ref, xnf_ref, dtf_ref, xpb_ref, xcb_ref, xnb_ref, dtb_in_ref,
                cw_ref, cb_ref, dtbias_ref, alog_ref, dskip_ref, expand_ref,
                yf_ref, yb_ref, state_ref, ext_ref, *, n_chunks, n_ctx_chunks):
    j = pl.program_id(1)

    @pl.when(j == 0)
    def _():
        state_ref[...] = jnp.zeros_like(state_ref)

    chunk_f = j
    chunk_b = jnp.where(j < n_ctx_chunks, n_ctx_chunks - 1 - j, n_chunks + n_ctx_chunks - 1 - j)
    common = (cw_ref, cb_ref, dtbias_ref, alog_ref, expand_ref, state_ref, ext_ref, n_chunks, n_ctx_chunks)
    slabs, xs = _ssd_direction(0, chunk_f, xpf_ref, xcf_ref, xnf_ref, dtf_ref, *common)
    for n, slab in enumerate(slabs):
        sl = slice(n * LANES, (n + 1) * LANES)
        yf_ref[:, sl] = slab + dskip_ref[:, sl] * xs[:, sl]
    slabs, _ = _ssd_direction(1, chunk_b, xpb_ref, xcb_ref, xnb_ref, dtb_in_ref, *common)
    for n, slab in enumerate(slabs):
        yb_ref[:, n * LANES:(n + 1) * LANES] = slab


def _ssd(xbc, dt, conv_w, conv_b, dt_bias, a_log, d_skip_e, expand):
    bsz, t_len, _ = xbc.shape
    n_chunks = t_len // B_CHUNK
    n_ctx_chunks = CTX_LEN // B_CHUNK
    rows8 = B_CHUNK // SUBLANES
    n_rows8 = t_len // SUBLANES

    def chunk_b(j):
        return jnp.where(j < n_ctx_chunks, n_ctx_chunks - 1 - j, n_chunks + n_ctx_chunks - 1 - j)

    def specs(chunk_of):
        halo = lambda fn: pl.BlockSpec((None, SUBLANES, B_XBC), fn)
        return [
            halo(lambda b, j: (b, jnp.maximum(chunk_of(j) * rows8 - 1, 0), 0)),
            pl.BlockSpec((None, B_CHUNK, B_XBC), lambda b, j: (b, chunk_of(j), 0)),
            halo(lambda b, j: (b, jnp.minimum((chunk_of(j) + 1) * rows8, n_rows8 - 1), 0)),
            pl.BlockSpec((None, B_CHUNK, LANES), lambda b, j: (b, chunk_of(j), 0)),
        ]

    const = lambda shape: pl.BlockSpec(shape, lambda b, j: (0,) * len(shape))
    y_spec = lambda chunk_of: pl.BlockSpec((None, B_CHUNK, B_INNER), lambda b, j: (b, chunk_of(j), 0))
    y_shape = jax.ShapeDtypeStruct((bsz, t_len, B_INNER), F32)
    return pl.pallas_call(
        functools.partial(_ssd_kernel, n_chunks=n_chunks, n_ctx_chunks=n_ctx_chunks),
        grid=(bsz, n_chunks),
        in_specs=specs(lambda j: j) + specs(chunk_b) + [
            const((SUBLANES, B_XBC)), const((1, B_XBC)), const((1, LANES)), const((1, LANES)),
            const((1, B_INNER)), const((2, LANES, B_INNER)),
        ],
        out_specs=[y_spec(lambda j: j), y_spec(chunk_b)],
        out_shape=[y_shape, y_shape],
        scratch_shapes=[
            pltpu.VMEM((2, B_GROUPS, B_STATE, B_HPG * B_HEAD_DIM), F32),
            pltpu.VMEM((B_CHUNK + 2 * SUBLANES, B_XBC), F32),
        ],
        compiler_params=_params(("parallel", "arbitrary")),
        name="ssd_scan",
    )(xbc, xbc, xbc, dt, xbc, xbc, xbc, dt, conv_w, conv_b, dt_bias, a_log, d_skip_e, expand)


def _outproj_even_kernel(att_ref, ga_ref, yf_ref, yb_ref, z_ref, tok_ref, mod_ref, sn_ref, w_ref, o_ref):
    att = (att_ref[...] * _silu(ga_ref[...])).astype(BF16)
    y = (yf_ref[...] + yb_ref[...]) * _silu(z_ref[...])
    ms = jnp.mean(y * y, axis=-1, keepdims=True)
    yn = (y * lax.rsqrt(ms + NORM_EPS) * sn_ref[...]).astype(BF16)
    out = _dot(att, w_ref[:A_WIDTH, :]) + _dot(yn, w_ref[A_WIDTH:, :])
    o_ref[...] = tok_ref[...] + mod_ref[2:3, :] * out


def _outproj_even(att, ga, yf, yb, z, tok, mods, ssm_norm, w):
    bsz, t_len, _ = tok.shape
    row = lambda width: pl.BlockSpec((None, ROW_TILE, width), lambda b, i: (b, i, 0))
    const = lambda shape: pl.BlockSpec(shape, lambda b, i: (0,) * len(shape))
    return pl.pallas_call(
        _outproj_even_kernel,
        grid=(bsz, t_len // ROW_TILE),
        in_specs=[
            row(A_WIDTH), row(A_WIDTH), row(B_INNER), row(B_INNER), row(B_INNER), row(D_MODEL),
            pl.BlockSpec((None, None, SUBLANES, D_MODEL), lambda b, i: (b, jnp.minimum(i, 1), 0, 0)),
            const((1, B_INNER)), const(w.shape),
        ],
        out_specs=row(D_MODEL),
        out_shape=jax.ShapeDtypeStruct((bsz, t_len, D_MODEL), F32),
        compiler_params=_params(("parallel", "parallel")),
        name="outproj_even",
    )(att, ga, yf, yb, z, tok, mods, ssm_norm.reshape(1, B_INNER), w)


def _diff_attn_kernel(lam_ref, q_ref, k_ref, v_ref, hn_ref, o_ref, s_ref, acc_ref, m_ref,
                      *, key_tile, lambda_init):
    tq = q_ref.shape[0]
    n_tiles = k_ref.shape[0] // key_tile
    lane = lax.broadcasted_iota(jnp.int32, (tq, LANES), 1)
    q = q_ref[...]
    zero = jnp.zeros_like(q)
    qz = [jnp.where((lane // HEAD_DIM) == c, q, zero) for c in range(2)]
    ones = jnp.ones((key_tile, LANES), BF16)

    def scores(t, slot):
        start = pl.multiple_of(t * key_tile, key_tile)
        k = k_ref[pl.ds(start, key_tile), :]
        for c in range(2):
            s_ref[slot, c] = _dot_nt(qz[c], k)

    def consume(t, slot):
        start = pl.multiple_of(t * key_tile, key_tile)
        v1 = jnp.concatenate([v_ref[pl.ds(start, key_tile), :], ones], axis=1)
        for c in range(2):
            s = s_ref[slot, c]
            m_old = m_ref[c]
            m_new = jnp.maximum(m_old, jnp.max(s, axis=-1, keepdims=True))
            alpha = jnp.exp2(m_old - m_new)
            p = jnp.exp2(s - m_new).astype(BF16)
            acc_ref[c] = alpha * acc_ref[c] + _dot(p, v1)
            m_ref[c] = m_new

    m_ref[...] = jnp.full(m_ref.shape, NEG_INF, F32)
    acc_ref[...] = jnp.zeros(acc_ref.shape, F32)
    scores(0, 0)

    def body(tt, carry):
        scores(2 * tt + 1, 1)
        consume(2 * tt, 0)
        scores(2 * tt + 2, 0)
        consume(2 * tt + 1, 1)
        return carry

    lax.fori_loop(0, (n_tiles - 1) // 2, body, 0)
    consume(n_tiles - 1, 0)

    lp = lam_ref[...]
    lam = (jnp.exp(jnp.sum(lp[0:1] * lp[1:2], axis=-1, keepdims=True))
           - jnp.exp(jnp.sum(lp[2:3] * lp[3:4], axis=-1, keepdims=True)) + lambda_init)
    a0 = acc_ref[0]
    a1 = acc_ref[1]
    o = a0[:, :LANES] * (1.0 / a0[:, LANES:]) - lam * (a1[:, :LANES] * (1.0 / a1[:, LANES:]))
    ms = jnp.mean(o * o, axis=-1, keepdims=True)
    o_ref[...] = o * lax.rsqrt(ms + NORM_EPS) * hn_ref[...] * (1.0 - lambda_init)


def _diff_attn(lam_params, q, k, v, head_norm, lambda_init):
    bsz, seq, _ = q.shape
    t_len = k.shape[1]
    q_tile = ATTN_Q_TILE
    key_tile = ATTN_KEY_TILE
    assert seq % q_tile == 0 and t_len % key_tile == 0 and (t_len // key_tile) % 2 == 1
    return pl.pallas_call(
        functools.partial(_diff_attn_kernel, key_tile=key_tile, lambda_init=lambda_init),
        grid=(bsz, C_HEADS, seq // q_tile),
        in_specs=[
            pl.BlockSpec(lam_params.shape, lambda b, h, i: (0, 0)),
            pl.BlockSpec((None, q_tile, C_VDIM), lambda b, h, i: (b, i, h)),
            pl.BlockSpec((None, t_len, C_VDIM), lambda b, h, i: (b, 0, h)),
            pl.BlockSpec((None, t_len, C_VDIM), lambda b, h, i: (b, 0, h)),
            pl.BlockSpec((1, C_VDIM), lambda b, h, i: (0, 0)),
        ],
        out_specs=pl.BlockSpec((None, q_tile, C_VDIM), lambda b, h, i: (b, i, h)),
        out_shape=jax.ShapeDtypeStruct((bsz, seq, C_WIDTH), F32),
        scratch_shapes=[
            pltpu.VMEM((2, 2, q_tile, key_tile), F32),
            pltpu.VMEM((2, q_tile, 2 * LANES), F32),
            pltpu.VMEM((2, q_tile, 1), F32),
        ],
        compiler_params=_params(("parallel", "parallel", "parallel")),
        name="diff_attention",
    )(lam_params, q, k, v, head_norm.reshape(1, C_VDIM))


def _outproj_odd_kernel(o_ref_in, g_ref, tok_ref, mod_ref, w_ref, o_ref):
    mix = (o_ref_in[...] * _silu(g_ref[...])).astype(BF16)
    o_ref[...] = tok_ref[...] + mod_ref[2:3, :] * _dot(mix, w_ref[...])


def _outproj_odd(o, g, tok, mods, w):
    bsz, seq, _ = o.shape
    ctx_tiles = CTX_LEN // ROW_TILE
    lat = lambda width: pl.BlockSpec((None, ROW_TILE, width), lambda b, i: (b, i, 0))
    cat = lambda width: pl.BlockSpec((None, ROW_TILE, width), lambda b, i: (b, i + ctx_tiles, 0))
    return pl.pallas_call(
        _outproj_odd_kernel,
        grid=(bsz, seq // ROW_TILE),
        in_specs=[
            lat(C_WIDTH), cat(C_WIDTH), cat(D_MODEL),
            pl.BlockSpec((None, None, SUBLANES, D_MODEL), lambda b, i: (b, 1, 0, 0)),
            pl.BlockSpec(w.shape, lambda b, i: (0, 0)),
        ],
        out_specs=lat(D_MODEL),
        out_shape=jax.ShapeDtypeStruct((bsz, seq, D_MODEL), F32),
        compiler_params=_params(("parallel", "parallel")),
        name="outproj_odd",
    )(o, g, tok, mods, w)


def _rope_tables(seq):
    rows = seq // GRID_W
    row = jnp.repeat(jnp.arange(rows), GRID_W).astype(F32)
    col = jnp.tile(jnp.arange(GRID_W), rows).astype(F32)
    n_freq = HEAD_DIM // 4
    inv_freq = ROPE_BASE ** (-jnp.arange(n_freq, dtype=F32) / n_freq)
    ang_r = row[:, None] * inv_freq
    ang_c = col[:, None] * inv_freq
    ang = jnp.concatenate([ang_r, ang_r, ang_c, ang_c], axis=-1)
    sign = jnp.tile(jnp.concatenate([-jnp.ones((n_freq,), F32), jnp.ones((n_freq,), F32)]), 2)
    cos = jnp.concatenate([jnp.ones((CTX_LEN, HEAD_DIM), F32), jnp.cos(ang)], axis=0)
    sin = jnp.concatenate([jnp.zeros((CTX_LEN, HEAD_DIM), F32), jnp.sin(ang) * sign], axis=0)
    reps = LANES // HEAD_DIM
    return jnp.tile(cos, (1, reps)), jnp.tile(sin, (1, reps))


def _pad_lanes(v):
    return jnp.pad(v, ((0, 0), (0, LANES - v.shape[1])))


def kernel(x, c, ctx, c_ctx, mod_w, mod_b, norm_w, ev_w_in, ev_w_out, ev_q_norm, ev_k_norm, ev_sink,
           ev_conv_w, ev_conv_b, ev_dt_bias, ev_a_log, ev_d_skip, ev_ssm_norm, od_w_in, od_w_out,
           od_q_norm, od_k_norm, od_lambda, od_head_norm):
    assert mod_w.shape[0] == DEPTH == 2
    bsz, seq, _ = x.shape
    assert ctx.shape[1] == CTX_LEN and seq % ROW_TILE == 0 and CTX_LEN == ROW_TILE

    cc = jnp.concatenate([c, c_ctx[None], jnp.zeros((SUBLANES - bsz - 1, D_MODEL), F32)], axis=0)
    mod_all = _modulation(cc, mod_w, mod_b)

    def mod_table(li):
        m = mod_all[li].reshape(SUBLANES, 3, D_MODEL)
        rows = jnp.stack([jnp.broadcast_to(m[bsz], (bsz, 3, D_MODEL)), m[:bsz]], axis=1)
        return jnp.pad(rows, ((0, 0), (0, 0), (0, SUBLANES - 3), (0, 0)))

    tok = jnp.concatenate([ctx, x], axis=1)
    cos, sin_signed = _rope_tables(seq)
    seg = jnp.arange(LANES) // HEAD_DIM
    seg_ones = (seg[:, None] == seg[None, :]).astype(BF16)
    tile_norm = lambda w: jnp.tile(w, LANES // HEAD_DIM).reshape(1, LANES)

    perm = jnp.array([(hh * A_GROUP + g) * HEAD_DIM + e
                      for g in range(A_GROUP) for hh in range(A_KV_HEADS) for e in range(HEAD_DIM)])
    w_in = ev_w_in[0]
    o_q, o_k, o_v, o_ga, o_z, o_xbc, o_dt = [int(s) for s in
        [0, A_WIDTH, A_WIDTH + A_KV_WIDTH, A_WIDTH + 2 * A_KV_WIDTH, 2 * A_WIDTH + 2 * A_KV_WIDTH,
         2 * A_WIDTH + 2 * A_KV_WIDTH + B_INNER, 2 * A_WIDTH + 2 * A_KV_WIDTH + B_INNER + B_XBC]]
    w0 = jnp.concatenate([
        w_in[:, o_q:o_k][:, perm], w_in[:, o_k:o_ga], w_in[:, o_ga:o_z][:, perm], w_in[:, o_z:],
        jnp.zeros((D_MODEL, LANES - 2 * B_HEADS), F32)], axis=1).astype(BF16)
    groups0 = ((o_q, A_WIDTH, "q", HEAD_DIM ** -0.5, False), (o_k, A_KV_WIDTH, "k", 1.0, False),
               (o_v, A_KV_WIDTH, "plain", 1.0, False), (o_ga, A_WIDTH, "plain", 1.0, False),
               (o_z, B_INNER, "plain", 1.0, False), (o_xbc, B_XBC, "plain", 1.0, False),
               (o_dt, LANES, "plain", 1.0, False))
    mods0 = mod_table(0)
    q, k, v, ga, z, xbc, dt = _inproj(
        tok, mods0, norm_w[0], w0, cos, sin_signed, tile_norm(ev_q_norm[0]), tile_norm(ev_k_norm[0]),
        seg_ones, groups0, (BF16, BF16, BF16, F32, F32, F32, F32))
    att = _win_attn(ev_sink[0], q, k, v)
    expand = jnp.stack([(jnp.arange(LANES)[:, None] == d * B_HEADS + jnp.arange(B_INNER)[None, :] // B_HEAD_DIM)
                        for d in range(2)]).astype(BF16)
    conv_w = jnp.pad(ev_conv_w[0], ((0, SUBLANES - B_CONV), (0, 0)))
    yf, yb = _ssd(xbc, dt, conv_w, ev_conv_b[0].reshape(1, B_XBC),
                  _pad_lanes(ev_dt_bias[0].reshape(1, 2 * B_HEADS)),
                  _pad_lanes(ev_a_log[0].reshape(1, 2 * B_HEADS)),
                  jnp.repeat(ev_d_skip[0], B_HEAD_DIM).reshape(1, B_INNER), expand)
    w_out0 = jnp.concatenate([ev_w_out[0][:A_WIDTH][perm], ev_w_out[0][A_WIDTH:]], axis=0).astype(BF16)
    tok = _outproj_even(att, ga, yf, yb, z, tok, mods0, ev_ssm_norm[0], w_out0)

    lambda_init = 0.8 - 0.6 * math.exp(-0.3 * 1)
    groups1 = ((0, C_WIDTH, "q", HEAD_DIM ** -0.5 * LOG2E, True), (C_WIDTH, C_WIDTH, "k", 1.0, False),
               (2 * C_WIDTH, C_WIDTH, "plain", 1.0, False), (3 * C_WIDTH, C_WIDTH, "plain", 1.0, False))
    mods1 = mod_table(1)
    q, k, v, g = _inproj(
        tok, mods1, norm_w[1], od_w_in[0].astype(BF16), cos, sin_signed, tile_norm(od_q_norm[0]),
        tile_norm(od_k_norm[0]), seg_ones, groups1, (BF16, BF16, BF16, F32))
    o = _diff_attn(od_lambda[0], q, k, v, od_head_norm[0], lambda_init)
    return _outproj_odd(o, g, tok, mods1, od_w_out[0].astype(BF16))
```

```python
import functools
import math
from typing import NamedTuple

import jax
import jax.numpy as jnp
from jax import lax
from jax.experimental import pallas as pl
from jax.experimental.pallas import tpu as pltpu

D_MODEL = 1024
DEPTH = 2
GRID_W = 64
CTX_LEN = 256
HEAD_DIM = 64
ROPE_BASE = 10000.0
NORM_EPS = 1e-6
BLOCK = 128

A_HEADS = 8
A_KV_HEADS = 2
A_GROUP = A_HEADS // A_KV_HEADS
A_WIDTH = A_HEADS * HEAD_DIM
A_KV_WIDTH = A_KV_HEADS * HEAD_DIM

B_INNER = D_MODEL
B_HEAD_DIM = 64
B_HEADS = B_INNER // B_HEAD_DIM
B_GROUPS = 2
B_HPG = B_HEADS // B_GROUPS
B_STATE = 128
B_CONV = 5
B_CHUNK = 128
B_XBC = B_INNER + 2 * B_GROUPS * B_STATE

C_HEADS = D_MODEL // (2 * HEAD_DIM)
C_VDIM = 2 * HEAD_DIM
C_WIDTH = C_HEADS * C_VDIM

LANES = 128
SUBLANES = 8
ROW_TILE = 256
ATTN_Q_TILE = 512
ATTN_KEY_TILE = 768
VMEM_LIMIT = 56 * 1024 * 1024

F32 = jnp.float32
BF16 = jnp.bfloat16
NEG_INF = float("-inf")
LOG2E = math.log2(math.e)


def _silu(x):
    return x * (1.0 / (1.0 + jnp.exp(-x)))


def _dot(a, b):
    return jnp.dot(a, b, preferred_element_type=F32)


def _dot_nt(a, b):
    return lax.dot_general(a, b, (((1,), (1,)), ((), ())), preferred_element_type=F32)


def _split2(x):
    hi = x.astype(BF16)
    lo = (x - hi.astype(F32)).astype(BF16)
    return hi, lo


def _split3(x):
    hi = x.astype(BF16)
    r = x - hi.astype(F32)
    mid = r.astype(BF16)
    lo = (r - mid.astype(F32)).astype(BF16)
    return hi, mid, lo


def _params(sem):
    return pltpu.CompilerParams(dimension_semantics=sem, vmem_limit_bytes=VMEM_LIMIT)


def _mod_kernel(cc_ref, w_ref, b_ref, o_ref):
    s = _silu(cc_ref[...])
    o_ref[...] = jnp.dot(s, w_ref[...], precision=lax.Precision.HIGHEST,
                         preferred_element_type=F32) + b_ref[...]


def _modulation(cc, mod_w, mod_b):
    depth = mod_w.shape[0]
    n_tiles = 3
    return pl.pallas_call(
        _mod_kernel,
        grid=(depth, n_tiles),
        in_specs=[
            pl.BlockSpec((SUBLANES, D_MODEL), lambda l, n: (0, 0)),
            pl.BlockSpec((None, D_MODEL, D_MODEL), lambda l, n: (l, 0, n)),
            pl.BlockSpec((None, 1, D_MODEL), lambda l, n: (l, 0, n)),
        ],
        out_specs=pl.BlockSpec((None, SUBLANES, D_MODEL), lambda l, n: (l, 0, n)),
        out_shape=jax.ShapeDtypeStruct((depth, SUBLANES, 3 * D_MODEL), F32),
        compiler_params=_params(("arbitrary", "arbitrary")),
        name="modulation",
    )(cc, mod_w, mod_b.reshape(depth, 1, 3 * D_MODEL))


class _Out(NamedTuple):
    start: int
    width: int
    dtype: object
    kind: str = "plain"
    scale: float = 1.0
    latent_only: bool = False
    transposed: bool = False


def _head_norm_rope(a, nw, cos, sin_signed, seg_ones2, first_quarter):
    hi, lo = _split2(a * a)
    ssq = _dot(jnp.concatenate([hi, lo], axis=1), seg_ones2)
    n = a * lax.rsqrt(ssq * (1.0 / HEAD_DIM) + NORM_EPS) * nw
    quarter = HEAD_DIM // 4
    rot = jnp.where(first_quarter, pltpu.roll(n, LANES - quarter, 1), pltpu.roll(n, quarter, 1))
    return n * cos + rot * sin_signed


def _inproj_kernel(ctx_ref, lat_ref, mod_ref, nw_ref, w_ref, cos_ref, sin_ref, qn_ref, kn_ref, ones_ref,
                   *out_refs, outs):
    x = jnp.where(pl.program_id(1) < CTX_LEN // ROW_TILE, ctx_ref[...], lat_ref[...])
    ms = jnp.mean(x * x, axis=-1, keepdims=True)
    y = x * lax.rsqrt(ms + NORM_EPS) * nw_ref[...]
    h = y * (1.0 + mod_ref[1:2, :]) + mod_ref[0:1, :]
    hb = h.astype(BF16)
    lane = lax.broadcasted_iota(jnp.int32, (x.shape[0], LANES), 1)
    first_quarter = (lane % (HEAD_DIM // 2)) < (HEAD_DIM // 4)
    for out, o_ref in zip(outs, out_refs):
        if out.kind == "plain" and not out.transposed:
            o_ref[...] = _dot(hb, w_ref[:, out.start:out.start + out.width]).astype(out.dtype)
            continue
        for s in range(out.width // LANES):
            cols = slice(s * LANES, (s + 1) * LANES)
            r = _dot(hb, w_ref[:, out.start + s * LANES:out.start + (s + 1) * LANES])
            if out.kind != "plain":
                nw = qn_ref[...] if out.kind == "q" else kn_ref[...]
                r = _head_norm_rope(r, nw, cos_ref[...], sin_ref[...], ones_ref[...], first_quarter) * out.scale
            if out.transposed:
                o_ref[cols, :] = r.astype(out.dtype).T
            else:
                o_ref[:, cols] = r.astype(out.dtype)


def _inproj(ctx_src, lat_src, mods, norm_w, w, cos, sin_signed, q_norm, k_norm, seg_ones2, outs):
    bsz = lat_src.shape[0]
    ctx_tiles = CTX_LEN // ROW_TILE
    split = ctx_src is not lat_src
    t_len = CTX_LEN + lat_src.shape[1] if split else lat_src.shape[1]
    n_row = t_len // ROW_TILE
    lat_row = (lambda i: jnp.maximum(i - ctx_tiles, 0)) if split else (lambda i: i)
    const = lambda shape: pl.BlockSpec(shape, lambda b, i: (0,) * len(shape))

    def out_spec(out):
        r = (lambda i: jnp.maximum(i - ctx_tiles, 0)) if out.latent_only else (lambda i: i)
        if out.transposed:
            return pl.BlockSpec((None, out.width, ROW_TILE), lambda b, i: (b, 0, r(i)))
        return pl.BlockSpec((None, ROW_TILE, out.width), lambda b, i: (b, r(i), 0))

    def out_shape(out):
        rows = t_len - CTX_LEN if out.latent_only else t_len
        return jax.ShapeDtypeStruct((bsz, out.width, rows) if out.transposed else (bsz, rows, out.width), out.dtype)

    return pl.pallas_call(
        functools.partial(_inproj_kernel, outs=outs),
        grid=(bsz, n_row),
        in_specs=[
            pl.BlockSpec((None, ROW_TILE, D_MODEL), lambda b, i: (b, 0, 0)),
            pl.BlockSpec((None, ROW_TILE, D_MODEL), lambda b, i: (b, lat_row(i), 0)),
            pl.BlockSpec((None, None, SUBLANES, D_MODEL), lambda b, i: (b, jnp.minimum(i, 1), 0, 0)),
            const((1, D_MODEL)),
            const(w.shape),
            pl.BlockSpec((ROW_TILE, LANES), lambda b, i: (i, 0)),
            pl.BlockSpec((ROW_TILE, LANES), lambda b, i: (i, 0)),
            const((1, LANES)),
            const((1, LANES)),
            const((2 * LANES, LANES)),
        ],
        out_specs=[out_spec(o) for o in outs],
        out_shape=[out_shape(o) for o in outs],
        compiler_params=_params(("parallel", "arbitrary")),
        name="inproj",
    )(ctx_src, lat_src, mods, norm_w.reshape(1, D_MODEL), w, cos, sin_signed, q_norm, k_norm, seg_ones2)


def _win_attn_kernel(sink_ref, q_ref, kp_ref, kc_ref, kn_ref, kx_ref, vp_ref, vc_ref, vn_ref, vx_ref,
                     bias_ref, o_ref):
    kk = jnp.concatenate([kp_ref[...], kc_ref[...], kn_ref[...], kx_ref[...]], axis=0)
    vv = jnp.concatenate([vp_ref[...], vc_ref[...], vn_ref[...], vx_ref[...]], axis=0)
    n_keys = kk.shape[0]
    v1 = jnp.concatenate([vv, jnp.ones((n_keys, LANES), BF16)], axis=1)
    rows = A_GROUP * BLOCK
    bias = bias_ref[...]
    lane = lax.broadcasted_iota(jnp.int32, (BLOCK, LANES), 1)
    head_of_lane = lane // HEAD_DIM
    g_of_row = lax.broadcasted_iota(jnp.int32, (rows, 1), 0) // BLOCK
    outs = []
    for hh in range(A_KV_HEADS):
        qs = jnp.concatenate(
            [jnp.where(head_of_lane == hh, q_ref[:, g * LANES:(g + 1) * LANES], jnp.zeros((BLOCK, LANES), BF16))
             for g in range(A_GROUP)], axis=0)
        s = _dot_nt(qs, kk).reshape(A_GROUP, BLOCK, n_keys) + bias[None]
        s = s.reshape(rows, n_keys)
        sink = jnp.zeros((rows, 1), F32)
        for g in range(A_GROUP):
            sink = jnp.where(g_of_row == g, sink_ref[hh * A_GROUP + g] * LOG2E, sink)
        m = jnp.maximum(jnp.max(s, axis=-1, keepdims=True), sink)
        e = jnp.exp2(s - m).astype(BF16)
        ov = _dot(e, v1)
        den = jnp.exp2(sink - m) + ov[:, LANES:]
        outs.append(ov[:, :LANES] * (1.0 / den))
    for g in range(A_GROUP):
        o_ref[:, g * LANES:(g + 1) * LANES] = jnp.where(
            head_of_lane == 0, outs[0][g * BLOCK:(g + 1) * BLOCK], outs[1][g * BLOCK:(g + 1) * BLOCK])


def _win_bias():
    t = jnp.arange(BLOCK)[:, None]
    c = jnp.arange(3 * BLOCK + CTX_LEN)[None, :]
    piece = c // BLOCK
    offset = c - t - BLOCK
    window = (offset >= -BLOCK) & (offset <= BLOCK)
    ctx_keys = jnp.broadcast_to(piece >= 3, window.shape)
    kinds = [
        ctx_keys,
        ctx_keys | (window & (piece >= 1) & (piece < 3)),
        ctx_keys | (window & (piece < 3)),
        ctx_keys | (window & (piece < 2)),
    ]
    return jnp.where(jnp.stack(kinds), 0.0, NEG_INF).astype(F32)


def _win_attn(sink, q, k, v):
    bsz, t_len, _ = q.shape
    n_blocks = t_len // BLOCK
    n_ctx_blocks = CTX_LEN // BLOCK
    assert n_blocks - n_ctx_blocks >= 2
    kv = lambda fn: pl.BlockSpec((None, BLOCK, A_KV_WIDTH), fn)
    prev = lambda b, i: (b, jnp.maximum(i - 1, 0), 0)
    cur = lambda b, i: (b, i, 0)
    nxt = lambda b, i: (b, jnp.minimum(i + 1, n_blocks - 1), 0)
    ctx = pl.BlockSpec((None, CTX_LEN, A_KV_WIDTH), lambda b, i: (b, 0, 0))

    def kind(b, i):
        latent = jnp.where(i == n_ctx_blocks, 1, jnp.where(i == n_blocks - 1, 3, 2))
        return (jnp.where(i < n_ctx_blocks, 0, latent), 0, 0)

    return pl.pallas_call(
        _win_attn_kernel,
        grid=(bsz, n_blocks),
        in_specs=[
            pl.BlockSpec(memory_space=pltpu.SMEM),
            pl.BlockSpec((None, BLOCK, A_WIDTH), cur),
            kv(prev), kv(cur), kv(nxt), ctx,
            kv(prev), kv(cur), kv(nxt), ctx,
            pl.BlockSpec((None, BLOCK, 3 * BLOCK + CTX_LEN), kind),
        ],
        out_specs=pl.BlockSpec((None, BLOCK, A_WIDTH), cur),
        out_shape=jax.ShapeDtypeStruct((bsz, t_len, A_WIDTH), F32),
        compiler_params=_params(("parallel", "parallel")),
        name="window_attention",
    )(sink, q, k, k, k, k, v, v, v, v, _win_bias())


def _ssd_direction(d, chunk, xp_ref, xc_ref, xn_ref, dt_ref, cw_ref, cb_ref, dtb_ref, alog_ref,
                   expand_ref, state_ref, ext_ref, n_chunks, n_ctx_chunks):
    backward = d == 1
    seq_first = (chunk == 0) | (chunk == n_ctx_chunks)
    seq_last = (chunk == n_ctx_chunks - 1) | (chunk == n_chunks - 1)
    ext_ref[0:SUBLANES, :] = xp_ref[...] * jnp.where(seq_first, 0.0, 1.0)
    ext_ref[SUBLANES:SUBLANES + B_CHUNK, :] = xc_ref[...]
    ext_ref[SUBLANES + B_CHUNK:2 * SUBLANES + B_CHUNK, :] = xn_ref[...] * jnp.where(seq_last, 0.0, 1.0)
    conv = cb_ref[...]
    for j in range(B_CONV):
        off = SUBLANES - B_CONV // 2 + j
        conv = conv + ext_ref[off:off + B_CHUNK, :] * cw_ref[j:j + 1, :]
    act = _silu(conv)
    xs = act[:, :B_INNER]

    lane = lax.broadcasted_iota(jnp.int32, (B_CHUNK, LANES), 1)
    row = lax.broadcasted_iota(jnp.int32, (B_CHUNK, LANES), 0)
    dir_lanes = (lane >= d * B_HEADS) & (lane < (d + 1) * B_HEADS)
    raw = dt_ref[...] + dtb_ref[...]
    dt = jnp.maximum(raw, 0.0) + jnp.log1p(jnp.exp(-jnp.abs(raw)))
    a = -jnp.exp(alog_ref[...])
    dta = jnp.where(dir_lanes, dt * a, 0.0)
    tri_mask = (lane >= row) if backward else (lane <= row)
    tri = jnp.where(tri_mask, 1.0, 0.0).astype(BF16)
    tri_t = jnp.where((row >= lane) if backward else (row <= lane), 1.0, 0.0).astype(BF16)
    la_col = sum(_dot(tri, p) for p in _split3(dta))
    la_row = sum(_dot(p, tri_t) for p in _split3(dta.T))
    last = 0 if backward else B_CHUNK - 1
    ela = jnp.exp(la_col)
    w_end = jnp.exp(la_col[last:last + 1, :] - la_col)
    expand = expand_ref[d]

    def widen(v):
        hi, lo = _split2(jnp.where(dir_lanes, v, 0.0))
        return _dot(hi, expand) + _dot(lo, expand)

    dt_e = widen(dt)
    ela_e = widen(ela)
    wdt_e = widen(w_end * dt)
    xdt = (xs * dt_e).astype(BF16)
    wx = (xs * wdt_e).astype(BF16)
    half_of_lane = lane // B_HEAD_DIM
    slabs = []
    for g in range(B_GROUPS):
        b_g = act[:, B_INNER + g * B_STATE:B_INNER + (g + 1) * B_STATE]
        c_g = act[:, B_INNER + B_GROUPS * B_STATE + g * B_STATE:
                  B_INNER + B_GROUPS * B_STATE + (g + 1) * B_STATE].astype(BF16)
        cb = _dot_nt(c_g, b_g.astype(BF16))
        gw = B_HPG * B_HEAD_DIM
        h_in = state_ref[d, g]
        y_inter = _dot(c_g, h_in.astype(BF16)) * ela_e[:, g * gw:(g + 1) * gw]
        for pair in range(B_HPG // 2):
            col0 = g * gw + pair * LANES
            xdt_pair = xdt[:, col0:col0 + LANES]
            acc = y_inter[:, pair * LANES:(pair + 1) * LANES]
            for e in range(2):
                col = d * B_HEADS + g * B_HPG + 2 * pair + e
                seg = la_col[:, col:col + 1] - la_row[col:col + 1, :]
                decay = jnp.exp(jnp.where(tri_mask, seg, NEG_INF))
                m = (cb * decay).astype(BF16)
                rhs = jnp.where(half_of_lane == e, xdt_pair, jnp.zeros_like(xdt_pair))
                acc = acc + _dot(m, rhs)
            slabs.append(acc)
        chunk_decay = ela_e[last:last + 1, g * gw:(g + 1) * gw]
        state_ref[d, g] = h_in * chunk_decay + _dot(b_g.T.astype(BF16), wx[:, g * gw:(g + 1) * gw])
    return slabs, xs


def _ssd_kernel(xpf_ref, xcf_ref, xnf_ref, dtf_ref, xpb_ref, xcb_ref, xnb_ref, dtb_in_ref,
                cw_ref, cb_ref, dtbias_ref, alog_ref, dskip_ref, expand_ref,
                yf_ref, yb_ref, state_ref, ext_ref, *, n_chunks, n_ctx_chunks):
    j = pl.program_id(1)

    @pl.when(j == 0)
    def _():
        state_ref[...] = jnp.zeros_like(state_ref)

    chunk_f = j
    chunk_b = jnp.where(j < n_ctx_chunks, n_ctx_chunks - 1 - j, n_chunks + n_ctx_chunks - 1 - j)
    common = (cw_ref, cb_ref, dtbias_ref, alog_ref, expand_ref, state_ref, ext_ref, n_chunks, n_ctx_chunks)
    slabs, xs = _ssd_direction(0, chunk_f, xpf_ref, xcf_ref, xnf_ref, dtf_ref, *common)
    for n, slab in enumerate(slabs):
        sl = slice(n * LANES, (n + 1) * LANES)
        yf_ref[:, sl] = slab + dskip_ref[:, sl] * xs[:, sl]
    slabs, _ = _ssd_direction(1, chunk_b, xpb_ref, xcb_ref, xnb_ref, dtb_in_ref, *common)
    for n, slab in enumerate(slabs):
        yb_ref[:, n * LANES:(n + 1) * LANES] = slab


def _ssd(xbc, dt, conv_w, conv_b, dt_bias, a_log, d_skip_e, expand):
    bsz, t_len, _ = xbc.shape
    n_chunks = t_len // B_CHUNK
    n_ctx_chunks = CTX_LEN // B_CHUNK
    rows8 = B_CHUNK // SUBLANES
    n_rows8 = t_len // SUBLANES

    def chunk_b(j):
        return jnp.where(j < n_ctx_chunks, n_ctx_chunks - 1 - j, n_chunks + n_ctx_chunks - 1 - j)

    def specs(chunk_of):
        halo = lambda fn: pl.BlockSpec((None, SUBLANES, B_XBC), fn)
        return [
            halo(lambda b, j: (b, jnp.maximum(chunk_of(j) * rows8 - 1, 0), 0)),
            pl.BlockSpec((None, B_CHUNK, B_XBC), lambda b, j: (b, chunk_of(j), 0)),
            halo(lambda b, j: (b, jnp.minimum((chunk_of(j) + 1) * rows8, n_rows8 - 1), 0)),
            pl.BlockSpec((None, B_CHUNK, LANES), lambda b, j: (b, chunk_of(j), 0)),
        ]

    const = lambda shape: pl.BlockSpec(shape, lambda b, j: (0,) * len(shape))
    y_spec = lambda chunk_of: pl.BlockSpec((None, B_CHUNK, B_INNER), lambda b, j: (b, chunk_of(j), 0))
    y_shape = jax.ShapeDtypeStruct((bsz, t_len, B_INNER), F32)
    return pl.pallas_call(
        functools.partial(_ssd_kernel, n_chunks=n_chunks, n_ctx_chunks=n_ctx_chunks),
        grid=(bsz, n_chunks),
        in_specs=specs(lambda j: j) + specs(chunk_b) + [
            const((SUBLANES, B_XBC)), const((1, B_XBC)), const((1, LANES)), const((1, LANES)),
            const((1, B_INNER)), const((2, LANES, B_INNER)),
        ],
        out_specs=[y_spec(lambda j: j), y_spec(chunk_b)],
        out_shape=[y_shape, y_shape],
        scratch_shapes=[
            pltpu.VMEM((2, B_GROUPS, B_STATE, B_HPG * B_HEAD_DIM), F32),
            pltpu.VMEM((B_CHUNK + 2 * SUBLANES, B_XBC), F32),
        ],
        compiler_params=_params(("parallel", "arbitrary")),
        name="ssd_scan",
    )(xbc, xbc, xbc, dt, xbc, xbc, xbc, dt, conv_w, conv_b, dt_bias, a_log, d_skip_e, expand)


def _outproj_even_kernel(att_ref, ga_ref, yf_ref, yb_ref, z_ref, ctx_ref, lat_ref, mod_ref, sn_ref, w_ref, o_ref):
    tok = jnp.where(pl.program_id(1) < CTX_LEN // ROW_TILE, ctx_ref[...], lat_ref[...])
    att = (att_ref[...] * _silu(ga_ref[...])).astype(BF16)
    y = (yf_ref[...] + yb_ref[...]) * _silu(z_ref[...])
    ms = jnp.mean(y * y, axis=-1, keepdims=True)
    yn = (y * lax.rsqrt(ms + NORM_EPS) * sn_ref[...]).astype(BF16)
    out = _dot(att, w_ref[:A_WIDTH, :]) + _dot(yn, w_ref[A_WIDTH:, :])
    o_ref[...] = tok + mod_ref[2:3, :] * out


def _outproj_even(att, ga, yf, yb, z, ctx, x, mods, ssm_norm, w):
    bsz, t_len, _ = att.shape
    ctx_tiles = CTX_LEN // ROW_TILE
    row = lambda width: pl.BlockSpec((None, ROW_TILE, width), lambda b, i: (b, i, 0))
    const = lambda shape: pl.BlockSpec(shape, lambda b, i: (0,) * len(shape))
    return pl.pallas_call(
        _outproj_even_kernel,
        grid=(bsz, t_len // ROW_TILE),
        in_specs=[
            row(A_WIDTH), row(A_WIDTH), row(B_INNER), row(B_INNER), row(B_INNER),
            pl.BlockSpec((None, ROW_TILE, D_MODEL), lambda b, i: (b, 0, 0)),
            pl.BlockSpec((None, ROW_TILE, D_MODEL), lambda b, i: (b, jnp.maximum(i - ctx_tiles, 0), 0)),
            pl.BlockSpec((None, None, SUBLANES, D_MODEL), lambda b, i: (b, jnp.minimum(i, 1), 0, 0)),
            const((1, B_INNER)), const(w.shape),
        ],
        out_specs=row(D_MODEL),
        out_shape=jax.ShapeDtypeStruct((bsz, t_len, D_MODEL), F32),
        compiler_params=_params(("parallel", "parallel")),
        name="outproj_even",
    )(att, ga, yf, yb, z, ctx, x, mods, ssm_norm.reshape(1, B_INNER), w)


ONES_ROWS = 16


def _diff_attn_kernel(lam_ref, qt_ref, k_ref, vt_ref, hn_ref, o_ref, s_ref, acc_ref, m_ref,
                      *, key_tile, lambda_init):
    tq = qt_ref.shape[1]
    n_tiles = k_ref.shape[0] // key_tile
    row = lax.broadcasted_iota(jnp.int32, (LANES, tq), 0)
    qt = qt_ref[...]
    zero = jnp.zeros_like(qt)
    qz = [jnp.where((row // HEAD_DIM) == c, qt, zero) for c in range(2)]
    ones = jnp.ones((ONES_ROWS, key_tile), BF16)

    def scores(t, slot):
        start = pl.multiple_of(t * key_tile, key_tile)
        k = k_ref[pl.ds(start, key_tile), :]
        for c in range(2):
            s_ref[slot, c] = _dot(k, qz[c])

    def consume(t, slot):
        start = pl.multiple_of(t * key_tile, key_tile)
        v1 = jnp.concatenate([vt_ref[:, pl.ds(start, key_tile)], ones], axis=0)
        for c in range(2):
            s = s_ref[slot, c]
            m_old = m_ref[c]
            m_new = jnp.maximum(m_old, jnp.max(s, axis=0, keepdims=True))
            alpha = jnp.exp2(m_old - m_new)
            p = jnp.exp2(s - m_new).astype(BF16)
            acc_ref[c] = alpha * acc_ref[c] + _dot(v1, p)
            m_ref[c] = m_new

    m_ref[...] = jnp.full(m_ref.shape, NEG_INF, F32)
    acc_ref[...] = jnp.zeros(acc_ref.shape, F32)
    scores(0, 0)

    def body(tt, carry):
        scores(2 * tt + 1, 1)
        consume(2 * tt, 0)
        scores(2 * tt + 2, 0)
        consume(2 * tt + 1, 1)
        return carry

    lax.fori_loop(0, (n_tiles - 1) // 2, body, 0)
    consume(n_tiles - 1, 0)

    lp = lam_ref[...]
    lam = (jnp.exp(jnp.sum(lp[0:1] * lp[1:2], axis=-1, keepdims=True))
           - jnp.exp(jnp.sum(lp[2:3] * lp[3:4], axis=-1, keepdims=True)) + lambda_init)
    a0 = acc_ref[0]
    a1 = acc_ref[1]
    o = a0[:LANES] * (1.0 / a0[LANES:LANES + 1]) - lam * (a1[:LANES] * (1.0 / a1[LANES:LANES + 1]))
    ms = jnp.mean(o * o, axis=0, keepdims=True)
    o = o * lax.rsqrt(ms + NORM_EPS) * hn_ref[...] * (1.0 - lambda_init)
    o_ref[...] = o.T


def _diff_attn(lam_params, qt, k, vt, head_norm, lambda_init):
    bsz, _, seq = qt.shape
    t_len = k.shape[1]
    q_tile = ATTN_Q_TILE
    key_tile = ATTN_KEY_TILE
    assert seq % q_tile == 0 and t_len % key_tile == 0 and (t_len // key_tile) % 2 == 1
    return pl.pallas_call(
        functools.partial(_diff_attn_kernel, key_tile=key_tile, lambda_init=lambda_init),
        grid=(bsz, C_HEADS, seq // q_tile),
        in_specs=[
            pl.BlockSpec(lam_params.shape, lambda b, h, i: (0, 0)),
            pl.BlockSpec((None, C_VDIM, q_tile), lambda b, h, i: (b, h, i)),
            pl.BlockSpec((None, t_len, C_VDIM), lambda b, h, i: (b, 0, h)),
            pl.BlockSpec((None, C_VDIM, t_len), lambda b, h, i: (b, h, 0)),
            pl.BlockSpec((C_VDIM, 1), lambda b, h, i: (0, 0)),
        ],
        out_specs=pl.BlockSpec((None, q_tile, C_VDIM), lambda b, h, i: (b, i, h)),
        out_shape=jax.ShapeDtypeStruct((bsz, seq, C_WIDTH), F32),
        scratch_shapes=[
            pltpu.VMEM((2, 2, key_tile, q_tile), F32),
            pltpu.VMEM((2, LANES + ONES_ROWS, q_tile), F32),
            pltpu.VMEM((2, 1, q_tile), F32),
        ],
        compiler_params=_params(("parallel", "parallel", "parallel")),
        name="diff_attention",
    )(lam_params, qt, k, vt, head_norm.reshape(C_VDIM, 1))


def _outproj_odd_kernel(o_ref_in, g_ref, tok_ref, mod_ref, w_ref, o_ref):
    mix = (o_ref_in[...] * _silu(g_ref[...])).astype(BF16)
    o_ref[...] = tok_ref[...] + mod_ref[2:3, :] * _dot(mix, w_ref[...])


def _outproj_odd(o, g, tok, mods, w):
    bsz, seq, _ = o.shape
    ctx_tiles = CTX_LEN // ROW_TILE
    lat = lambda width: pl.BlockSpec((None, ROW_TILE, width), lambda b, i: (b, i, 0))
    cat = lambda width: pl.BlockSpec((None, ROW_TILE, width), lambda b, i: (b, i + ctx_tiles, 0))
    return pl.pallas_call(
        _outproj_odd_kernel,
        grid=(bsz, seq // ROW_TILE),
        in_specs=[
            lat(C_WIDTH), cat(C_WIDTH), cat(D_MODEL),
            pl.BlockSpec((None, None, SUBLANES, D_MODEL), lambda b, i: (b, 1, 0, 0)),
            pl.BlockSpec(w.shape, lambda b, i: (0, 0)),
        ],
        out_specs=lat(D_MODEL),
        out_shape=jax.ShapeDtypeStruct((bsz, seq, D_MODEL), F32),
        compiler_params=_params(("parallel", "parallel")),
        name="outproj_odd",
    )(o, g, tok, mods, w)


def _rope_tables(seq):
    rows = seq // GRID_W
    row = jnp.repeat(jnp.arange(rows), GRID_W).astype(F32)
    col = jnp.tile(jnp.arange(GRID_W), rows).astype(F32)
    n_freq = HEAD_DIM // 4
    inv_freq = ROPE_BASE ** (-jnp.arange(n_freq, dtype=F32) / n_freq)
    ang_r = row[:, None] * inv_freq
    ang_c = col[:, None] * inv_freq
    ang = jnp.concatenate([ang_r, ang_r, ang_c, ang_c], axis=-1)
    sign = jnp.tile(jnp.concatenate([-jnp.ones((n_freq,), F32), jnp.ones((n_freq,), F32)]), 2)
    cos = jnp.concatenate([jnp.ones((CTX_LEN, HEAD_DIM), F32), jnp.cos(ang)], axis=0)
    sin = jnp.concatenate([jnp.zeros((CTX_LEN, HEAD_DIM), F32), jnp.sin(ang) * sign], axis=0)
    reps = LANES // HEAD_DIM
    return jnp.tile(cos, (1, reps)), jnp.tile(sin, (1, reps))


def _pad_lanes(v):
    return jnp.pad(v, ((0, 0), (0, LANES - v.shape[1])))


def kernel(x, c, ctx, c_ctx, mod_w, mod_b, norm_w, ev_w_in, ev_w_out, ev_q_norm, ev_k_norm, ev_sink,
           ev_conv_w, ev_conv_b, ev_dt_bias, ev_a_log, ev_d_skip, ev_ssm_norm, od_w_in, od_w_out,
           od_q_norm, od_k_norm, od_lambda, od_head_norm):
    assert mod_w.shape[0] == DEPTH == 2
    bsz, seq, _ = x.shape
    assert ctx.shape[1] == CTX_LEN and seq % ROW_TILE == 0 and CTX_LEN == ROW_TILE

    cc = jnp.concatenate([c, c_ctx[None], jnp.zeros((SUBLANES - bsz - 1, D_MODEL), F32)], axis=0)
    mod_all = _modulation(cc, mod_w, mod_b)

    def mod_table(li):
        m = mod_all[li].reshape(SUBLANES, 3, D_MODEL)
        rows = jnp.stack([jnp.broadcast_to(m[bsz], (bsz, 3, D_MODEL)), m[:bsz]], axis=1)
        return jnp.pad(rows, ((0, 0), (0, 0), (0, SUBLANES - 3), (0, 0)))

    cos, sin_signed = _rope_tables(seq)
    seg = jnp.arange(LANES) // HEAD_DIM
    seg_ones = (seg[:, None] == seg[None, :]).astype(BF16)
    seg_ones2 = jnp.concatenate([seg_ones, seg_ones], axis=0)
    tile_norm = lambda w: jnp.tile(w, LANES // HEAD_DIM).reshape(1, LANES)

    perm = jnp.array([(hh * A_GROUP + g) * HEAD_DIM + e
                      for g in range(A_GROUP) for hh in range(A_KV_HEADS) for e in range(HEAD_DIM)])
    w_in = ev_w_in[0]
    o_q, o_k, o_v, o_ga, o_z, o_xbc, o_dt = [int(s) for s in
        [0, A_WIDTH, A_WIDTH + A_KV_WIDTH, A_WIDTH + 2 * A_KV_WIDTH, 2 * A_WIDTH + 2 * A_KV_WIDTH,
         2 * A_WIDTH + 2 * A_KV_WIDTH + B_INNER, 2 * A_WIDTH + 2 * A_KV_WIDTH + B_INNER + B_XBC]]
    w0 = jnp.concatenate([
        w_in[:, o_q:o_k][:, perm], w_in[:, o_k:o_ga], w_in[:, o_ga:o_z][:, perm], w_in[:, o_z:],
        jnp.zeros((D_MODEL, LANES - 2 * B_HEADS), F32)], axis=1).astype(BF16)
    q_scale = HEAD_DIM ** -0.5 * LOG2E
    outs0 = (_Out(o_q, A_WIDTH, BF16, "q", q_scale), _Out(o_k, A_KV_WIDTH, BF16, "k"),
             _Out(o_v, A_KV_WIDTH, BF16), _Out(o_ga, A_WIDTH, F32), _Out(o_z, B_INNER, F32),
             _Out(o_xbc, B_XBC, F32), _Out(o_dt, LANES, F32))
    mods0 = mod_table(0)
    q, k, v, ga, z, xbc, dt = _inproj(
        ctx, x, mods0, norm_w[0], w0, cos, sin_signed, tile_norm(ev_q_norm[0]), tile_norm(ev_k_norm[0]),
        seg_ones2, outs0)
    att = _win_attn(ev_sink[0], q, k, v)
    expand = jnp.stack([(jnp.arange(LANES)[:, None] == d * B_HEADS + jnp.arange(B_INNER)[None, :] // B_HEAD_DIM)
                        for d in range(2)]).astype(BF16)
    conv_w = jnp.pad(ev_conv_w[0], ((0, SUBLANES - B_CONV), (0, 0)))
    yf, yb = _ssd(xbc, dt, conv_w, ev_conv_b[0].reshape(1, B_XBC),
                  _pad_lanes(ev_dt_bias[0].reshape(1, 2 * B_HEADS)),
                  _pad_lanes(ev_a_log[0].reshape(1, 2 * B_HEADS)),
                  jnp.repeat(ev_d_skip[0], B_HEAD_DIM).reshape(1, B_INNER), expand)
    w_out0 = jnp.concatenate([ev_w_out[0][:A_WIDTH][perm], ev_w_out[0][A_WIDTH:]], axis=0).astype(BF16)
    tok = _outproj_even(att, ga, yf, yb, z, ctx, x, mods0, ev_ssm_norm[0], w_out0)

    lambda_init = 0.8 - 0.6 * math.exp(-0.3 * 1)
    outs1 = (_Out(0, C_WIDTH, BF16, "q", q_scale, latent_only=True, transposed=True),
             _Out(C_WIDTH, C_WIDTH, BF16, "k"), _Out(2 * C_WIDTH, C_WIDTH, BF16, transposed=True),
             _Out(3 * C_WIDTH, C_WIDTH, F32))
    mods1 = mod_table(1)
    qt, k, vt, g = _inproj(
        tok, tok, mods1, norm_w[1], od_w_in[0].astype(BF16), cos, sin_signed, tile_norm(od_q_norm[0]),
        tile_norm(od_k_norm[0]), seg_ones2, outs1)
    o = _diff_attn(od_lambda[0], qt, k, vt, od_head_norm[0], lambda_init)
    return _outproj_odd(o, g, tok, mods1, od_w_out[0].astype(BF16))
```

```python
import functools
import math
from typing import NamedTuple

import jax
import jax.numpy as jnp
from jax import lax
from jax.experimental import pallas as pl
from jax.experimental.pallas import tpu as pltpu

D_MODEL = 1024
DEPTH = 2
GRID_W = 64
CTX_LEN = 256
HEAD_DIM = 64
ROPE_BASE = 10000.0
NORM_EPS = 1e-6
BLOCK = 128

A_HEADS = 8
A_KV_HEADS = 2
A_GROUP = A_HEADS // A_KV_HEADS
A_WIDTH = A_HEADS * HEAD_DIM
A_KV_WIDTH = A_KV_HEADS * HEAD_DIM

B_INNER = D_MODEL
B_HEAD_DIM = 64
B_HEADS = B_INNER // B_HEAD_DIM
B_GROUPS = 2
B_HPG = B_HEADS // B_GROUPS
B_STATE = 128
B_CONV = 5
B_CHUNK = 128
B_XBC = B_INNER + 2 * B_GROUPS * B_STATE

C_HEADS = D_MODEL // (2 * HEAD_DIM)
C_VDIM = 2 * HEAD_DIM
C_WIDTH = C_HEADS * C_VDIM

LANES = 128
SUBLANES = 8
ROW_TILE = 256
ATTN_Q_TILE = 512
ATTN_KEY_TILE = 768
VMEM_LIMIT = 56 * 1024 * 1024

F32 = jnp.float32
BF16 = jnp.bfloat16
NEG_INF = float("-inf")
LOG2E = math.log2(math.e)


def _silu(x):
    return x * (1.0 / (1.0 + jnp.exp(-x)))


def _dot(a, b):
    return jnp.dot(a, b, preferred_element_type=F32)


def _dot_nt(a, b):
    return lax.dot_general(a, b, (((1,), (1,)), ((), ())), preferred_element_type=F32)


def _split2(x):
    hi = x.astype(BF16)
    lo = (x - hi.astype(F32)).astype(BF16)
    return hi, lo


def _split3(x):
    hi = x.astype(BF16)
    r = x - hi.astype(F32)
    mid = r.astype(BF16)
    lo = (r - mid.astype(F32)).astype(BF16)
    return hi, mid, lo


def _params(sem):
    return pltpu.CompilerParams(dimension_semantics=sem, vmem_limit_bytes=VMEM_LIMIT)


def _mod_kernel(cc_ref, w_ref, b_ref, o_ref):
    s = _silu(cc_ref[...])
    o_ref[...] = jnp.dot(s, w_ref[...], precision=lax.Precision.HIGHEST,
                         preferred_element_type=F32) + b_ref[...]


def _modulation(cc, mod_w, mod_b):
    depth = mod_w.shape[0]
    n_tiles = 3
    return pl.pallas_call(
        _mod_kernel,
        grid=(depth, n_tiles),
        in_specs=[
            pl.BlockSpec((SUBLANES, D_MODEL), lambda l, n: (0, 0)),
            pl.BlockSpec((None, D_MODEL, D_MODEL), lambda l, n: (l, 0, n)),
            pl.BlockSpec((None, 1, D_MODEL), lambda l, n: (l, 0, n)),
        ],
        out_specs=pl.BlockSpec((None, SUBLANES, D_MODEL), lambda l, n: (l, 0, n)),
        out_shape=jax.ShapeDtypeStruct((depth, SUBLANES, 3 * D_MODEL), F32),
        compiler_params=_params(("arbitrary", "arbitrary")),
        name="modulation",
    )(cc, mod_w, mod_b.reshape(depth, 1, 3 * D_MODEL))


class _Out(NamedTuple):
    start: int
    width: int
    dtype: object
    kind: str = "plain"
    scale: float = 1.0
    latent_only: bool = False
    transposed: bool = False


def _head_norm_rope(a, nw, cos, sin_signed, seg_ones2, first_quarter):
    hi, lo = _split2(a * a)
    ssq = _dot(jnp.concatenate([hi, lo], axis=1), seg_ones2)
    n = a * lax.rsqrt(ssq * (1.0 / HEAD_DIM) + NORM_EPS) * nw
    quarter = HEAD_DIM // 4
    rot = jnp.where(first_quarter, pltpu.roll(n, LANES - quarter, 1), pltpu.roll(n, quarter, 1))
    return n * cos + rot * sin_signed


def _inproj_kernel(ctx_ref, lat_ref, mod_ref, nw_ref, w_ref, cos_ref, sin_ref, qn_ref, kn_ref, ones_ref,
                   *out_refs, outs):
    x = jnp.where(pl.program_id(1) < CTX_LEN // ROW_TILE, ctx_ref[...], lat_ref[...])
    ms = jnp.mean(x * x, axis=-1, keepdims=True)
    y = x * lax.rsqrt(ms + NORM_EPS) * nw_ref[...]
    h = y * (1.0 + mod_ref[1:2, :]) + mod_ref[0:1, :]
    hb = h.astype(BF16)
    lane = lax.broadcasted_iota(jnp.int32, (x.shape[0], LANES), 1)
    first_quarter = (lane % (HEAD_DIM // 2)) < (HEAD_DIM // 4)
    for out, o_ref in zip(outs, out_refs):
        wide = _dot(hb, w_ref[:, out.start:out.start + out.width])
        if out.kind == "plain" and not out.transposed:
            o_ref[...] = wide.astype(out.dtype)
            continue
        for s in range(out.width // LANES):
            cols = slice(s * LANES, (s + 1) * LANES)
            r = wide[:, cols]
            if out.kind != "plain":
                nw = qn_ref[...] if out.kind == "q" else kn_ref[...]
                r = _head_norm_rope(r, nw, cos_ref[...], sin_ref[...], ones_ref[...], first_quarter) * out.scale
            if out.transposed:
                o_ref[cols, :] = r.astype(out.dtype).T
            else:
                o_ref[:, cols] = r.astype(out.dtype)


def _inproj(ctx_src, lat_src, mods, norm_w, w, cos, sin_signed, q_norm, k_norm, seg_ones2, outs):
    bsz = lat_src.shape[0]
    ctx_tiles = CTX_LEN // ROW_TILE
    split = ctx_src is not lat_src
    t_len = CTX_LEN + lat_src.shape[1] if split else lat_src.shape[1]
    n_row = t_len // ROW_TILE
    lat_row = (lambda i: jnp.maximum(i - ctx_tiles, 0)) if split else (lambda i: i)
    const = lambda shape: pl.BlockSpec(shape, lambda b, i: (0,) * len(shape))

    def out_spec(out):
        r = (lambda i: jnp.maximum(i - ctx_tiles, 0)) if out.latent_only else (lambda i: i)
        if out.transposed:
            return pl.BlockSpec((None, out.width, ROW_TILE), lambda b, i: (b, 0, r(i)))
        return pl.BlockSpec((None, ROW_TILE, out.width), lambda b, i: (b, r(i), 0))

    def out_shape(out):
        rows = t_len - CTX_LEN if out.latent_only else t_len
        return jax.ShapeDtypeStruct((bsz, out.width, rows) if out.transposed else (bsz, rows, out.width), out.dtype)

    return pl.pallas_call(
        functools.partial(_inproj_kernel, outs=outs),
        grid=(bsz, n_row),
        in_specs=[
            pl.BlockSpec((None, ROW_TILE, D_MODEL), lambda b, i: (b, 0, 0)),
            pl.BlockSpec((None, ROW_TILE, D_MODEL), lambda b, i: (b, lat_row(i), 0)),
            pl.BlockSpec((None, None, SUBLANES, D_MODEL), lambda b, i: (b, jnp.minimum(i, 1), 0, 0)),
            const((1, D_MODEL)),
            const(w.shape),
            pl.BlockSpec((ROW_TILE, LANES), lambda b, i: (i, 0)),
            pl.BlockSpec((ROW_TILE, LANES), lambda b, i: (i, 0)),
            const((1, LANES)),
            const((1, LANES)),
            const((2 * LANES, LANES)),
        ],
        out_specs=[out_spec(o) for o in outs],
        out_shape=[out_shape(o) for o in outs],
        compiler_params=_params(("parallel", "arbitrary")),
        name="inproj",
    )(ctx_src, lat_src, mods, norm_w.reshape(1, D_MODEL), w, cos, sin_signed, q_norm, k_norm, seg_ones2)


def _win_attn_kernel(sink_ref, q_ref, kp_ref, kc_ref, kn_ref, kx_ref, vp_ref, vc_ref, vn_ref, vx_ref,
                     bias_ref, o_ref):
    kk = jnp.concatenate([kp_ref[...], kc_ref[...], kn_ref[...], kx_ref[...]], axis=0)
    vv = jnp.concatenate([vp_ref[...], vc_ref[...], vn_ref[...], vx_ref[...]], axis=0)
    n_keys = kk.shape[0]
    v1 = jnp.concatenate([vv, jnp.ones((n_keys, LANES), BF16)], axis=1)
    rows = A_GROUP * BLOCK
    bias = bias_ref[...]
    lane = lax.broadcasted_iota(jnp.int32, (BLOCK, LANES), 1)
    head_of_lane = lane // HEAD_DIM
    g_of_row = lax.broadcasted_iota(jnp.int32, (rows, 1), 0) // BLOCK
    outs = []
    for hh in range(A_KV_HEADS):
        qs = jnp.concatenate(
            [jnp.where(head_of_lane == hh, q_ref[:, g * LANES:(g + 1) * LANES], jnp.zeros((BLOCK, LANES), BF16))
             for g in range(A_GROUP)], axis=0)
        s = _dot_nt(qs, kk).reshape(A_GROUP, BLOCK, n_keys) + bias[None]
        s = s.reshape(rows, n_keys)
        sink = jnp.zeros((rows, 1), F32)
        for g in range(A_GROUP):
            sink = jnp.where(g_of_row == g, sink_ref[hh * A_GROUP + g] * LOG2E, sink)
        m = jnp.maximum(jnp.max(s, axis=-1, keepdims=True), sink)
        e = jnp.exp2(s - m).astype(BF16)
        ov = _dot(e, v1)
        den = jnp.exp2(sink - m) + ov[:, LANES:]
        outs.append(ov[:, :LANES] * (1.0 / den))
    for g in range(A_GROUP):
        o_ref[:, g * LANES:(g + 1) * LANES] = jnp.where(
            head_of_lane == 0, outs[0][g * BLOCK:(g + 1) * BLOCK], outs[1][g * BLOCK:(g + 1) * BLOCK])


def _win_bias():
    t = jnp.arange(BLOCK)[:, None]
    c = jnp.arange(3 * BLOCK + CTX_LEN)[None, :]
    piece = c // BLOCK
    offset = c - t - BLOCK
    window = (offset >= -BLOCK) & (offset <= BLOCK)
    ctx_keys = jnp.broadcast_to(piece >= 3, window.shape)
    kinds = [
        ctx_keys,
        ctx_keys | (window & (piece >= 1) & (piece < 3)),
        ctx_keys | (window & (piece < 3)),
        ctx_keys | (window & (piece < 2)),
    ]
    return jnp.where(jnp.stack(kinds), 0.0, NEG_INF).astype(F32)


def _win_attn(sink, q, k, v):
    bsz, t_len, _ = q.shape
    n_blocks = t_len // BLOCK
    n_ctx_blocks = CTX_LEN // BLOCK
    assert n_blocks - n_ctx_blocks >= 2
    kv = lambda fn: pl.BlockSpec((None, BLOCK, A_KV_WIDTH), fn)
    prev = lambda b, i: (b, jnp.maximum(i - 1, 0), 0)
    cur = lambda b, i: (b, i, 0)
    nxt = lambda b, i: (b, jnp.minimum(i + 1, n_blocks - 1), 0)
    ctx = pl.BlockSpec((None, CTX_LEN, A_KV_WIDTH), lambda b, i: (b, 0, 0))

    def kind(b, i):
        latent = jnp.where(i == n_ctx_blocks, 1, jnp.where(i == n_blocks - 1, 3, 2))
        return (jnp.where(i < n_ctx_blocks, 0, latent), 0, 0)

    return pl.pallas_call(
        _win_attn_kernel,
        grid=(bsz, n_blocks),
        in_specs=[
            pl.BlockSpec(memory_space=pltpu.SMEM),
            pl.BlockSpec((None, BLOCK, A_WIDTH), cur),
            kv(prev), kv(cur), kv(nxt), ctx,
            kv(prev), kv(cur), kv(nxt), ctx,
            pl.BlockSpec((None, BLOCK, 3 * BLOCK + CTX_LEN), kind),
        ],
        out_specs=pl.BlockSpec((None, BLOCK, A_WIDTH), cur),
        out_shape=jax.ShapeDtypeStruct((bsz, t_len, A_WIDTH), F32),
        compiler_params=_params(("parallel", "parallel")),
        name="window_attention",
    )(sink, q, k, k, k, k, v, v, v, v, _win_bias())


def _ssd_prep_kernel(xp_ref, xc_ref, xn_ref, dt_ref, cw_ref, cb_ref, dtb_ref,
                     xs_ref, c_ref, bt_ref, cbm_ref, dtsp_ref, ext_ref, *, n_chunks, n_ctx_chunks):
    chunk = pl.program_id(1)
    seq_first = (chunk == 0) | (chunk == n_ctx_chunks)
    seq_last = (chunk == n_ctx_chunks - 1) | (chunk == n_chunks - 1)
    ext_ref[0:SUBLANES, :] = xp_ref[...] * jnp.where(seq_first, 0.0, 1.0)
    ext_ref[SUBLANES:SUBLANES + B_CHUNK, :] = xc_ref[...]
    ext_ref[SUBLANES + B_CHUNK:2 * SUBLANES + B_CHUNK, :] = xn_ref[...] * jnp.where(seq_last, 0.0, 1.0)
    ext = ext_ref[...]
    n_ext = B_CHUNK + 2 * SUBLANES
    conv = cb_ref[...]
    for j in range(B_CONV):
        shifted = ext if j == B_CONV // 2 else pltpu.roll(ext, (B_CONV // 2 - j) % n_ext, 0)
        conv = conv + shifted[SUBLANES:SUBLANES + B_CHUNK, :] * cw_ref[j:j + 1, :]
    act = _silu(conv)
    xs_ref[...] = act[:, :B_INNER]
    for g in range(B_GROUPS):
        b_g = act[:, B_INNER + g * B_STATE:B_INNER + (g + 1) * B_STATE]
        c_g = act[:, B_INNER + (B_GROUPS + g) * B_STATE:B_INNER + (B_GROUPS + g + 1) * B_STATE].astype(BF16)
        c_ref[:, g * B_STATE:(g + 1) * B_STATE] = c_g
        bt_ref[g * B_STATE:(g + 1) * B_STATE, :] = b_g.T.astype(BF16)
        cbm_ref[g] = _dot_nt(c_g, b_g.astype(BF16))
    raw = dt_ref[...] + dtb_ref[...]
    dtsp_ref[...] = jnp.maximum(raw, 0.0) + jnp.log(1.0 + jnp.exp(-jnp.abs(raw)))


def _ssd_prep(xbc, dt, conv_w, conv_b, dt_bias):
    bsz, t_len, _ = xbc.shape
    n_chunks = t_len // B_CHUNK
    n_ctx_chunks = CTX_LEN // B_CHUNK
    rows8 = B_CHUNK // SUBLANES
    n_rows8 = t_len // SUBLANES
    halo = lambda fn: pl.BlockSpec((None, SUBLANES, B_XBC), fn)
    const = lambda shape: pl.BlockSpec(shape, lambda b, j: (0,) * len(shape))
    rows = lambda width: pl.BlockSpec((None, B_CHUNK, width), lambda b, j: (b, j, 0))
    return pl.pallas_call(
        functools.partial(_ssd_prep_kernel, n_chunks=n_chunks, n_ctx_chunks=n_ctx_chunks),
        grid=(bsz, n_chunks),
        in_specs=[
            halo(lambda b, j: (b, jnp.maximum(j * rows8 - 1, 0), 0)),
            rows(B_XBC),
            halo(lambda b, j: (b, jnp.minimum((j + 1) * rows8, n_rows8 - 1), 0)),
            rows(LANES),
            const((SUBLANES, B_XBC)), const((1, B_XBC)), const((1, LANES)),
        ],
        out_specs=[
            rows(B_INNER), rows(B_GROUPS * B_STATE),
            pl.BlockSpec((None, None, B_GROUPS * B_STATE, B_CHUNK), lambda b, j: (b, j, 0, 0)),
            pl.BlockSpec((None, None, B_GROUPS, B_CHUNK, B_CHUNK), lambda b, j: (b, j, 0, 0, 0)),
            rows(LANES),
        ],
        out_shape=[
            jax.ShapeDtypeStruct((bsz, t_len, B_INNER), F32),
            jax.ShapeDtypeStruct((bsz, t_len, B_GROUPS * B_STATE), BF16),
            jax.ShapeDtypeStruct((bsz, n_chunks, B_GROUPS * B_STATE, B_CHUNK), BF16),
            jax.ShapeDtypeStruct((bsz, n_chunks, B_GROUPS, B_CHUNK, B_CHUNK), F32),
            jax.ShapeDtypeStruct((bsz, t_len, LANES), F32),
        ],
        scratch_shapes=[pltpu.VMEM((B_CHUNK + 2 * SUBLANES, B_XBC), F32)],
        compiler_params=_params(("parallel", "parallel")),
        name="ssd_prep",
    )(xbc, xbc, xbc, dt, conv_w, conv_b, dt_bias)


def _ssd_direction(d, xs_ref, c_ref, bt_ref, cbm_ref, dt_ref, alog_ref, expand_ref, state_ref):
    backward = d == 1
    xs = xs_ref[...]
    lane = lax.broadcasted_iota(jnp.int32, (B_CHUNK, LANES), 1)
    row = lax.broadcasted_iota(jnp.int32, (B_CHUNK, LANES), 0)
    dir_lanes = (lane >= d * B_HEADS) & (lane < (d + 1) * B_HEADS)
    dt = dt_ref[...]
    a = -jnp.exp(alog_ref[...])
    dta = jnp.where(dir_lanes, dt * a, 0.0)
    tri_mask = (lane >= row) if backward else (lane <= row)
    tri = jnp.where(tri_mask, 1.0, 0.0).astype(BF16)
    tri_t = jnp.where((row >= lane) if backward else (row <= lane), 1.0, 0.0).astype(BF16)
    la_col = sum(_dot(tri, p) for p in _split3(dta))
    la_row = sum(_dot(p, tri_t) for p in _split3(dta.T))
    last = 0 if backward else B_CHUNK - 1
    ela = jnp.exp(la_col)
    w_end = jnp.exp(la_col[last:last + 1, :] - la_col)
    expand2 = expand_ref[d]

    def widen(v):
        hi, lo = _split2(jnp.where(dir_lanes, v, 0.0))
        return _dot(jnp.concatenate([hi, lo], axis=1), expand2)

    dt_e = widen(dt)
    ela_e = widen(ela)
    wdt_e = widen(w_end * dt)
    xdt = (xs * dt_e).astype(BF16)
    wx = (xs * wdt_e).astype(BF16)
    half_of_lane = lane // B_HEAD_DIM
    slabs = []
    gw = B_HPG * B_HEAD_DIM
    for g in range(B_GROUPS):
        c_g = c_ref[:, g * B_STATE:(g + 1) * B_STATE]
        cb = cbm_ref[g]
        h_in = state_ref[d, g]
        y_inter = _dot(c_g, h_in.astype(BF16)) * ela_e[:, g * gw:(g + 1) * gw]
        for pair in range(B_HPG // 2):
            col0 = g * gw + pair * LANES
            xdt_pair = xdt[:, col0:col0 + LANES]
            acc = y_inter[:, pair * LANES:(pair + 1) * LANES]
            for e in range(2):
                col = d * B_HEADS + g * B_HPG + 2 * pair + e
                seg = la_col[:, col:col + 1] - la_row[col:col + 1, :]
                decay = jnp.exp(jnp.where(tri_mask, seg, NEG_INF))
                m = (cb * decay).astype(BF16)
                rhs = jnp.where(half_of_lane == e, xdt_pair, jnp.zeros_like(xdt_pair))
                acc = acc + _dot(m, rhs)
            slabs.append(acc)
        chunk_decay = ela_e[last:last + 1, g * gw:(g + 1) * gw]
        state_ref[d, g] = h_in * chunk_decay + _dot(bt_ref[g * B_STATE:(g + 1) * B_STATE, :], wx[:, g * gw:(g + 1) * gw])
    return slabs, xs


def _ssd_kernel(xsf_ref, cf_ref, btf_ref, cbf_ref, dtf_ref, xsb_ref, cb_ref, btb_ref, cbb_ref, dtb_ref,
                alog_ref, dskip_ref, expand_ref, yf_ref, yb_ref, state_ref):
    @pl.when(pl.program_id(1) == 0)
    def _():
        state_ref[...] = jnp.zeros_like(state_ref)

    common = (alog_ref, expand_ref, state_ref)
    slabs, xs = _ssd_direction(0, xsf_ref, cf_ref, btf_ref, cbf_ref, dtf_ref, *common)
    for n, slab in enumerate(slabs):
        sl = slice(n * LANES, (n + 1) * LANES)
        yf_ref[:, sl] = slab + dskip_ref[:, sl] * xs[:, sl]
    slabs, _ = _ssd_direction(1, xsb_ref, cb_ref, btb_ref, cbb_ref, dtb_ref, *common)
    for n, slab in enumerate(slabs):
        yb_ref[:, n * LANES:(n + 1) * LANES] = slab


def _ssd(xs, c, bt, cbm, dt, a_log, d_skip_e, expand2):
    bsz, t_len, _ = xs.shape
    n_chunks = t_len // B_CHUNK
    n_ctx_chunks = CTX_LEN // B_CHUNK

    def chunk_b(j):
        return jnp.where(j < n_ctx_chunks, n_ctx_chunks - 1 - j, n_chunks + n_ctx_chunks - 1 - j)

    def specs(chunk_of):
        return [
            pl.BlockSpec((None, B_CHUNK, B_INNER), lambda b, j: (b, chunk_of(j), 0)),
            pl.BlockSpec((None, B_CHUNK, B_GROUPS * B_STATE), lambda b, j: (b, chunk_of(j), 0)),
            pl.BlockSpec((None, None, B_GROUPS * B_STATE, B_CHUNK), lambda b, j: (b, chunk_of(j), 0, 0)),
            pl.BlockSpec((None, None, B_GROUPS, B_CHUNK, B_CHUNK), lambda b, j: (b, chunk_of(j), 0, 0, 0)),
            pl.BlockSpec((None, B_CHUNK, LANES), lambda b, j: (b, chunk_of(j), 0)),
        ]

    const = lambda shape: pl.BlockSpec(shape, lambda b, j: (0,) * len(shape))
    y_spec = lambda chunk_of: pl.BlockSpec((None, B_CHUNK, B_INNER), lambda b, j: (b, chunk_of(j), 0))
    y_shape = jax.ShapeDtypeStruct((bsz, t_len, B_INNER), F32)
    return pl.pallas_call(
        _ssd_kernel,
        grid=(bsz, n_chunks),
        in_specs=specs(lambda j: j) + specs(chunk_b) + [
            const((1, LANES)), const((1, B_INNER)), const((2, 2 * LANES, B_INNER)),
        ],
        out_specs=[y_spec(lambda j: j), y_spec(chunk_b)],
        out_shape=[y_shape, y_shape],
        scratch_shapes=[pltpu.VMEM((2, B_GROUPS, B_STATE, B_HPG * B_HEAD_DIM), F32)],
        compiler_params=_params(("parallel", "arbitrary")),
        name="ssd_scan",
    )(xs, c, bt, cbm, dt, xs, c, bt, cbm, dt, a_log, d_skip_e, expand2)


def _outproj_even_kernel(att_ref, ga_ref, yf_ref, yb_ref, z_ref, ctx_ref, lat_ref, mod_ref, sn_ref, w_ref, o_ref):
    tok = jnp.where(pl.program_id(1) < CTX_LEN // ROW_TILE, ctx_ref[...], lat_ref[...])
    att = (att_ref[...] * _silu(ga_ref[...])).astype(BF16)
    y = (yf_ref[...] + yb_ref[...]) * _silu(z_ref[...])
    ms = jnp.mean(y * y, axis=-1, keepdims=True)
    yn = (y * lax.rsqrt(ms + NORM_EPS) * sn_ref[...]).astype(BF16)
    out = _dot(att, w_ref[:A_WIDTH, :]) + _dot(yn, w_ref[A_WIDTH:, :])
    o_ref[...] = tok + mod_ref[2:3, :] * out


def _outproj_even(att, ga, yf, yb, z, ctx, x, mods, ssm_norm, w):
    bsz, t_len, _ = att.shape
    ctx_tiles = CTX_LEN // ROW_TILE
    row = lambda width: pl.BlockSpec((None, ROW_TILE, width), lambda b, i: (b, i, 0))
    const = lambda shape: pl.BlockSpec(shape, lambda b, i: (0,) * len(shape))
    return pl.pallas_call(
        _outproj_even_kernel,
        grid=(bsz, t_len // ROW_TILE),
        in_specs=[
            row(A_WIDTH), row(A_WIDTH), row(B_INNER), row(B_INNER), row(B_INNER),
            pl.BlockSpec((None, ROW_TILE, D_MODEL), lambda b, i: (b, 0, 0)),
            pl.BlockSpec((None, ROW_TILE, D_MODEL), lambda b, i: (b, jnp.maximum(i - ctx_tiles, 0), 0)),
            pl.BlockSpec((None, None, SUBLANES, D_MODEL), lambda b, i: (b, jnp.minimum(i, 1), 0, 0)),
            const((1, B_INNER)), const(w.shape),
        ],
        out_specs=row(D_MODEL),
        out_shape=jax.ShapeDtypeStruct((bsz, t_len, D_MODEL), F32),
        compiler_params=_params(("parallel", "parallel")),
        name="outproj_even",
    )(att, ga, yf, yb, z, ctx, x, mods, ssm_norm.reshape(1, B_INNER), w)


ONES_ROWS = 16


def _diff_attn_kernel(lam_ref, qt_ref, k_ref, vt_ref, hn_ref, o_ref, s_ref, acc_ref, m_ref,
                      *, key_tile, lambda_init):
    tq = qt_ref.shape[1]
    n_tiles = k_ref.shape[0] // key_tile
    row = lax.broadcasted_iota(jnp.int32, (LANES, tq), 0)
    qt = qt_ref[...]
    zero = jnp.zeros_like(qt)
    qz = [jnp.where((row // HEAD_DIM) == c, qt, zero) for c in range(2)]
    ones = jnp.ones((ONES_ROWS, key_tile), BF16)

    def scores(t, slot):
        start = pl.multiple_of(t * key_tile, key_tile)
        k = k_ref[pl.ds(start, key_tile), :]
        for c in range(2):
            s_ref[slot, c] = _dot(k, qz[c])

    def consume(t, slot):
        start = pl.multiple_of(t * key_tile, key_tile)
        v1 = jnp.concatenate([vt_ref[:, pl.ds(start, key_tile)], ones], axis=0)
        for c in range(2):
            s = s_ref[slot, c]
            m_old = m_ref[c]
            m_new = jnp.maximum(m_old, jnp.max(s, axis=0, keepdims=True))
            alpha = jnp.exp2(m_old - m_new)
            p = jnp.exp2(s - m_new).astype(BF16)
            acc_ref[c] = alpha * acc_ref[c] + _dot(v1, p)
            m_ref[c] = m_new

    m_ref[...] = jnp.full(m_ref.shape, NEG_INF, F32)
    acc_ref[...] = jnp.zeros(acc_ref.shape, F32)
    scores(0, 0)

    def body(tt, carry):
        scores(2 * tt + 1, 1)
        consume(2 * tt, 0)
        scores(2 * tt + 2, 0)
        consume(2 * tt + 1, 1)
        return carry

    lax.fori_loop(0, (n_tiles - 1) // 2, body, 0)
    consume(n_tiles - 1, 0)

    lp = lam_ref[...]
    lam = (jnp.exp(jnp.sum(lp[0:1] * lp[1:2], axis=-1, keepdims=True))
           - jnp.exp(jnp.sum(lp[2:3] * lp[3:4], axis=-1, keepdims=True)) + lambda_init)
    a0 = acc_ref[0]
    a1 = acc_ref[1]
    o = a0[:LANES] * (1.0 / a0[LANES:LANES + 1]) - lam * (a1[:LANES] * (1.0 / a1[LANES:LANES + 1]))
    ms = jnp.mean(o * o, axis=0, keepdims=True)
    o = o * lax.rsqrt(ms + NORM_EPS) * hn_ref[...] * (1.0 - lambda_init)
    o_ref[...] = o.T


def _diff_attn(lam_params, qt, k, vt, head_norm, lambda_init):
    bsz, _, seq = qt.shape
    t_len = k.shape[1]
    q_tile = ATTN_Q_TILE
    key_tile = ATTN_KEY_TILE
    assert seq % q_tile == 0 and t_len % key_tile == 0 and (t_len // key_tile) % 2 == 1
    return pl.pallas_call(
        functools.partial(_diff_attn_kernel, key_tile=key_tile, lambda_init=lambda_init),
        grid=(bsz, C_HEADS, seq // q_tile),
        in_specs=[
            pl.BlockSpec(lam_params.shape, lambda b, h, i: (0, 0)),
            pl.BlockSpec((None, C_VDIM, q_tile), lambda b, h, i: (b, h, i)),
            pl.BlockSpec((None, t_len, C_VDIM), lambda b, h, i: (b, 0, h)),
            pl.BlockSpec((None, C_VDIM, t_len), lambda b, h, i: (b, h, 0)),
            pl.BlockSpec((C_VDIM, 1), lambda b, h, i: (0, 0)),
        ],
        out_specs=pl.BlockSpec((None, q_tile, C_VDIM), lambda b, h, i: (b, i, h)),
        out_shape=jax.ShapeDtypeStruct((bsz, seq, C_WIDTH), F32),
        scratch_shapes=[
            pltpu.VMEM((2, 2, key_tile, q_tile), F32),
            pltpu.VMEM((2, LANES + ONES_ROWS, q_tile), F32),
            pltpu.VMEM((2, 1, q_tile), F32),
        ],
        compiler_params=_params(("parallel", "parallel", "parallel")),
        name="diff_attention",
    )(lam_params, qt, k, vt, head_norm.reshape(C_VDIM, 1))


def _outproj_odd_kernel(o_ref_in, g_ref, tok_ref, mod_ref, w_ref, o_ref):
    mix = (o_ref_in[...] * _silu(g_ref[...])).astype(BF16)
    o_ref[...] = tok_ref[...] + mod_ref[2:3, :] * _dot(mix, w_ref[...])


def _outproj_odd(o, g, tok, mods, w):
    bsz, seq, _ = o.shape
    ctx_tiles = CTX_LEN // ROW_TILE
    lat = lambda width: pl.BlockSpec((None, ROW_TILE, width), lambda b, i: (b, i, 0))
    cat = lambda width: pl.BlockSpec((None, ROW_TILE, width), lambda b, i: (b, i + ctx_tiles, 0))
    return pl.pallas_call(
        _outproj_odd_kernel,
        grid=(bsz, seq // ROW_TILE),
        in_specs=[
            lat(C_WIDTH), cat(C_WIDTH), cat(D_MODEL),
            pl.BlockSpec((None, None, SUBLANES, D_MODEL), lambda b, i: (b, 1, 0, 0)),
            pl.BlockSpec(w.shape, lambda b, i: (0, 0)),
        ],
        out_specs=lat(D_MODEL),
        out_shape=jax.ShapeDtypeStruct((bsz, seq, D_MODEL), F32),
        compiler_params=_params(("parallel", "parallel")),
        name="outproj_odd",
    )(o, g, tok, mods, w)


def _rope_tables(seq):
    rows = seq // GRID_W
    row = jnp.repeat(jnp.arange(rows), GRID_W).astype(F32)
    col = jnp.tile(jnp.arange(GRID_W), rows).astype(F32)
    n_freq = HEAD_DIM // 4
    inv_freq = ROPE_BASE ** (-jnp.arange(n_freq, dtype=F32) / n_freq)
    ang_r = row[:, None] * inv_freq
    ang_c = col[:, None] * inv_freq
    ang = jnp.concatenate([ang_r, ang_r, ang_c, ang_c], axis=-1)
    sign = jnp.tile(jnp.concatenate([-jnp.ones((n_freq,), F32), jnp.ones((n_freq,), F32)]), 2)
    cos = jnp.concatenate([jnp.ones((CTX_LEN, HEAD_DIM), F32), jnp.cos(ang)], axis=0)
    sin = jnp.concatenate([jnp.zeros((CTX_LEN, HEAD_DIM), F32), jnp.sin(ang) * sign], axis=0)
    reps = LANES // HEAD_DIM
    return jnp.tile(cos, (1, reps)), jnp.tile(sin, (1, reps))


def _pad_lanes(v):
    return jnp.pad(v, ((0, 0), (0, LANES - v.shape[1])))


def kernel(x, c, ctx, c_ctx, mod_w, mod_b, norm_w, ev_w_in, ev_w_out, ev_q_norm, ev_k_norm, ev_sink,
           ev_conv_w, ev_conv_b, ev_dt_bias, ev_a_log, ev_d_skip, ev_ssm_norm, od_w_in, od_w_out,
           od_q_norm, od_k_norm, od_lambda, od_head_norm):
    assert mod_w.shape[0] == DEPTH == 2
    bsz, seq, _ = x.shape
    assert ctx.shape[1] == CTX_LEN and seq % ROW_TILE == 0 and CTX_LEN == ROW_TILE

    cc = jnp.concatenate([c, c_ctx[None], jnp.zeros((SUBLANES - bsz - 1, D_MODEL), F32)], axis=0)
    mod_all = _modulation(cc, mod_w, mod_b)

    def mod_table(li):
        m = mod_all[li].reshape(SUBLANES, 3, D_MODEL)
        rows = jnp.stack([jnp.broadcast_to(m[bsz], (bsz, 3, D_MODEL)), m[:bsz]], axis=1)
        return jnp.pad(rows, ((0, 0), (0, 0), (0, SUBLANES - 3), (0, 0)))

    cos, sin_signed = _rope_tables(seq)
    seg = jnp.arange(LANES) // HEAD_DIM
    seg_ones = (seg[:, None] == seg[None, :]).astype(BF16)
    seg_ones2 = jnp.concatenate([seg_ones, seg_ones], axis=0)
    tile_norm = lambda w: jnp.tile(w, LANES // HEAD_DIM).reshape(1, LANES)

    perm = jnp.array([(hh * A_GROUP + g) * HEAD_DIM + e
                      for g in range(A_GROUP) for hh in range(A_KV_HEADS) for e in range(HEAD_DIM)])
    w_in = ev_w_in[0]
    o_q, o_k, o_v, o_ga, o_z, o_xbc, o_dt = [int(s) for s in
        [0, A_WIDTH, A_WIDTH + A_KV_WIDTH, A_WIDTH + 2 * A_KV_WIDTH, 2 * A_WIDTH + 2 * A_KV_WIDTH,
         2 * A_WIDTH + 2 * A_KV_WIDTH + B_INNER, 2 * A_WIDTH + 2 * A_KV_WIDTH + B_INNER + B_XBC]]
    w0 = jnp.concatenate([
        w_in[:, o_q:o_k][:, perm], w_in[:, o_k:o_ga], w_in[:, o_ga:o_z][:, perm], w_in[:, o_z:],
        jnp.zeros((D_MODEL, LANES - 2 * B_HEADS), F32)], axis=1).astype(BF16)
    q_scale = HEAD_DIM ** -0.5 * LOG2E
    outs0 = (_Out(o_q, A_WIDTH, BF16, "q", q_scale), _Out(o_k, A_KV_WIDTH, BF16, "k"),
             _Out(o_v, A_KV_WIDTH, BF16), _Out(o_ga, A_WIDTH, F32), _Out(o_z, B_INNER, F32),
             _Out(o_xbc, B_XBC, F32), _Out(o_dt, LANES, F32))
    mods0 = mod_table(0)
    q, k, v, ga, z, xbc, dt = _inproj(
        ctx, x, mods0, norm_w[0], w0, cos, sin_signed, tile_norm(ev_q_norm[0]), tile_norm(ev_k_norm[0]),
        seg_ones2, outs0)
    att = _win_attn(ev_sink[0], q, k, v)
    expand = jnp.stack([(jnp.arange(LANES)[:, None] == d * B_HEADS + jnp.arange(B_INNER)[None, :] // B_HEAD_DIM)
                        for d in range(2)]).astype(BF16)
    expand2 = jnp.concatenate([expand, expand], axis=1)
    conv_w = jnp.pad(ev_conv_w[0], ((0, SUBLANES - B_CONV), (0, 0)))
    xs, c_bf, b_t, c_bt, dt_sp = _ssd_prep(xbc, dt, conv_w, ev_conv_b[0].reshape(1, B_XBC),
                                           _pad_lanes(ev_dt_bias[0].reshape(1, 2 * B_HEADS)))
    yf, yb = _ssd(xs, c_bf, b_t, c_bt, dt_sp, _pad_lanes(ev_a_log[0].reshape(1, 2 * B_HEADS)),
                  jnp.repeat(ev_d_skip[0], B_HEAD_DIM).reshape(1, B_INNER), expand2)
    w_out0 = jnp.concatenate([ev_w_out[0][:A_WIDTH][perm], ev_w_out[0][A_WIDTH:]], axis=0).astype(BF16)
    tok = _outproj_even(att, ga, yf, yb, z, ctx, x, mods0, ev_ssm_norm[0], w_out0)

    lambda_init = 0.8 - 0.6 * math.exp(-0.3 * 1)
    outs1 = (_Out(0, C_WIDTH, BF16, "q", q_scale, latent_only=True, transposed=True),
             _Out(C_WIDTH, C_WIDTH, BF16, "k"), _Out(2 * C_WIDTH, C_WIDTH, BF16, transposed=True),
             _Out(3 * C_WIDTH, C_WIDTH, F32))
    mods1 = mod_table(1)
    qt, k, vt, g = _inproj(
        tok, tok, mods1, norm_w[1], od_w_in[0].astype(BF16), cos, sin_signed, tile_norm(od_q_norm[0]),
        tile_norm(od_k_norm[0]), seg_ones2, outs1)
    o = _diff_attn(od_lambda[0], qt, k, vt, od_head_norm[0], lambda_init)
    return _outproj_odd(o, g, tok, mods1, od_w_out[0].astype(BF16))
```

```python
import functools
import math
from typing import NamedTuple

import jax
import jax.numpy as jnp
from jax import lax
from jax.experimental import pallas as pl
from jax.experimental.pallas import tpu as pltpu

D_MODEL = 1024
DEPTH = 2
GRID_W = 64
CTX_LEN = 256
HEAD_DIM = 64
ROPE_BASE = 10000.0
NORM_EPS = 1e-6
BLOCK = 128

A_HEADS = 8
A_KV_HEADS = 2
A_GROUP = A_HEADS // A_KV_HEADS
A_WIDTH = A_HEADS * HEAD_DIM
A_KV_WIDTH = A_KV_HEADS * HEAD_DIM

B_INNER = D_MODEL
B_HEAD_DIM = 64
B_HEADS = B_INNER // B_HEAD_DIM
B_GROUPS = 2
B_HPG = B_HEADS // B_GROUPS
B_STATE = 128
B_CONV = 5
B_CHUNK = 128
B_XBC = B_INNER + 2 * B_GROUPS * B_STATE

C_HEADS = D_MODEL // (2 * HEAD_DIM)
C_VDIM = 2 * HEAD_DIM
C_WIDTH = C_HEADS * C_VDIM

LANES = 128
SUBLANES = 8
ROW_TILE = 256
ATTN_Q_TILE = 2048
ATTN_KEY_TILE = 768
VMEM_LIMIT = 56 * 1024 * 1024

F32 = jnp.float32
BF16 = jnp.bfloat16
NEG_INF = float("-inf")
LOG2E = math.log2(math.e)


def _silu(x):
    return x * (1.0 / (1.0 + jnp.exp(-x)))


def _dot(a, b):
    return jnp.dot(a, b, preferred_element_type=F32)


def _dot_nt(a, b):
    return lax.dot_general(a, b, (((1,), (1,)), ((), ())), preferred_element_type=F32)


def _split2(x):
    hi = x.astype(BF16)
    lo = (x - hi.astype(F32)).astype(BF16)
    return hi, lo


def _split3(x):
    hi = x.astype(BF16)
    r = x - hi.astype(F32)
    mid = r.astype(BF16)
    lo = (r - mid.astype(F32)).astype(BF16)
    return hi, mid, lo


def _params(sem):
    return pltpu.CompilerParams(dimension_semantics=sem, vmem_limit_bytes=VMEM_LIMIT)


def _mod_kernel(cc_ref, w_ref, b_ref, o_ref):
    s = _silu(cc_ref[...])
    o_ref[...] = jnp.dot(s, w_ref[...], precision=lax.Precision.HIGHEST,
                         preferred_element_type=F32) + b_ref[...]


def _modulation(cc, mod_w, mod_b):
    depth = mod_w.shape[0]
    n_tiles = 3
    return pl.pallas_call(
        _mod_kernel,
        grid=(depth, n_tiles),
        in_specs=[
            pl.BlockSpec((SUBLANES, D_MODEL), lambda l, n: (0, 0)),
            pl.BlockSpec((None, D_MODEL, D_MODEL), lambda l, n: (l, 0, n)),
            pl.BlockSpec((None, 1, D_MODEL), lambda l, n: (l, 0, n)),
        ],
        out_specs=pl.BlockSpec((None, SUBLANES, D_MODEL), lambda l, n: (l, 0, n)),
        out_shape=jax.ShapeDtypeStruct((depth, SUBLANES, 3 * D_MODEL), F32),
        compiler_params=_params(("arbitrary", "arbitrary")),
        name="modulation",
    )(cc, mod_w, mod_b.reshape(depth, 1, 3 * D_MODEL))


class _Out(NamedTuple):
    start: int
    width: int
    dtype: object
    kind: str = "plain"
    scale: float = 1.0
    latent_only: bool = False
    transposed: bool = False


def _head_norm_rope(a, nw, cos, sin_signed, seg_ones2, first_quarter):
    hi, lo = _split2(a * a)
    ssq = _dot(jnp.concatenate([hi, lo], axis=1), seg_ones2)
    n = a * lax.rsqrt(ssq * (1.0 / HEAD_DIM) + NORM_EPS) * nw
    quarter = HEAD_DIM // 4
    rot = jnp.where(first_quarter, pltpu.roll(n, LANES - quarter, 1), pltpu.roll(n, quarter, 1))
    return n * cos + rot * sin_signed


def _inproj_kernel(ctx_ref, lat_ref, mod_ref, nw_ref, w_ref, cos_ref, sin_ref, qn_ref, kn_ref, ones_ref,
                   *out_refs, outs):
    x = jnp.where(pl.program_id(1) < CTX_LEN // ROW_TILE, ctx_ref[...], lat_ref[...])
    ms = jnp.mean(x * x, axis=-1, keepdims=True)
    y = x * lax.rsqrt(ms + NORM_EPS) * nw_ref[...]
    h = y * (1.0 + mod_ref[1:2, :]) + mod_ref[0:1, :]
    hb = h.astype(BF16)
    lane = lax.broadcasted_iota(jnp.int32, (x.shape[0], LANES), 1)
    first_quarter = (lane % (HEAD_DIM // 2)) < (HEAD_DIM // 4)
    for out, o_ref in zip(outs, out_refs):
        wide = _dot(hb, w_ref[:, out.start:out.start + out.width])
        if out.kind == "plain" and not out.transposed:
            o_ref[...] = wide.astype(out.dtype)
            continue
        for s in range(out.width // LANES):
            cols = slice(s * LANES, (s + 1) * LANES)
            r = wide[:, cols]
            if out.kind != "plain":
                nw = qn_ref[...] if out.kind == "q" else kn_ref[...]
                r = _head_norm_rope(r, nw, cos_ref[...], sin_ref[...], ones_ref[...], first_quarter) * out.scale
            if out.transposed:
                o_ref[cols, :] = r.astype(out.dtype).T
            else:
                o_ref[:, cols] = r.astype(out.dtype)


def _inproj(ctx_src, lat_src, mods, norm_w, w, cos, sin_signed, q_norm, k_norm, seg_ones2, outs):
    bsz = lat_src.shape[0]
    ctx_tiles = CTX_LEN // ROW_TILE
    split = ctx_src is not lat_src
    t_len = CTX_LEN + lat_src.shape[1] if split else lat_src.shape[1]
    n_row = t_len // ROW_TILE
    lat_row = (lambda i: jnp.maximum(i - ctx_tiles, 0)) if split else (lambda i: i)
    const = lambda shape: pl.BlockSpec(shape, lambda b, i: (0,) * len(shape))

    def out_spec(out):
        r = (lambda i: jnp.maximum(i - ctx_tiles, 0)) if out.latent_only else (lambda i: i)
        if out.transposed:
            return pl.BlockSpec((None, out.width, ROW_TILE), lambda b, i: (b, 0, r(i)))
        return pl.BlockSpec((None, ROW_TILE, out.width), lambda b, i: (b, r(i), 0))

    def out_shape(out):
        rows = t_len - CTX_LEN if out.latent_only else t_len
        return jax.ShapeDtypeStruct((bsz, out.width, rows) if out.transposed else (bsz, rows, out.width), out.dtype)

    return pl.pallas_call(
        functools.partial(_inproj_kernel, outs=outs),
        grid=(bsz, n_row),
        in_specs=[
            pl.BlockSpec((None, ROW_TILE, D_MODEL), lambda b, i: (b, 0, 0)),
            pl.BlockSpec((None, ROW_TILE, D_MODEL), lambda b, i: (b, lat_row(i), 0)),
            pl.BlockSpec((None, None, SUBLANES, D_MODEL), lambda b, i: (b, jnp.minimum(i, 1), 0, 0)),
            const((1, D_MODEL)),
            const(w.shape),
            pl.BlockSpec((ROW_TILE, LANES), lambda b, i: (i, 0)),
            pl.BlockSpec((ROW_TILE, LANES), lambda b, i: (i, 0)),
            const((1, LANES)),
            const((1, LANES)),
            const((2 * LANES, LANES)),
        ],
        out_specs=[out_spec(o) for o in outs],
        out_shape=[out_shape(o) for o in outs],
        compiler_params=_params(("parallel", "arbitrary")),
        name="inproj",
    )(ctx_src, lat_src, mods, norm_w.reshape(1, D_MODEL), w, cos, sin_signed, q_norm, k_norm, seg_ones2)


def _win_attn_kernel(sink_ref, q_ref, kp_ref, kc_ref, kn_ref, kx_ref, vp_ref, vc_ref, vn_ref, vx_ref,
                     bias_ref, o_ref):
    kk = jnp.concatenate([kp_ref[...], kc_ref[...], kn_ref[...], kx_ref[...]], axis=0)
    vv = jnp.concatenate([vp_ref[...], vc_ref[...], vn_ref[...], vx_ref[...]], axis=0)
    n_keys = kk.shape[0]
    v1 = jnp.concatenate([vv, jnp.ones((n_keys, LANES), BF16)], axis=1)
    rows = A_GROUP * BLOCK
    bias = bias_ref[...]
    lane = lax.broadcasted_iota(jnp.int32, (BLOCK, LANES), 1)
    head_of_lane = lane // HEAD_DIM
    g_of_row = lax.broadcasted_iota(jnp.int32, (rows, 1), 0) // BLOCK
    outs = []
    for hh in range(A_KV_HEADS):
        qs = jnp.concatenate(
            [jnp.where(head_of_lane == hh, q_ref[:, g * LANES:(g + 1) * LANES], jnp.zeros((BLOCK, LANES), BF16))
             for g in range(A_GROUP)], axis=0)
        s = _dot_nt(qs, kk).reshape(A_GROUP, BLOCK, n_keys) + bias[None]
        s = s.reshape(rows, n_keys)
        sink = jnp.zeros((rows, 1), F32)
        for g in range(A_GROUP):
            sink = jnp.where(g_of_row == g, sink_ref[hh * A_GROUP + g] * LOG2E, sink)
        m = jnp.maximum(jnp.max(s, axis=-1, keepdims=True), sink)
        e = jnp.exp2(s - m).astype(BF16)
        ov = _dot(e, v1)
        den = jnp.exp2(sink - m) + ov[:, LANES:]
        outs.append(ov[:, :LANES] * (1.0 / den))
    for g in range(A_GROUP):
        o_ref[:, g * LANES:(g + 1) * LANES] = jnp.where(
            head_of_lane == 0, outs[0][g * BLOCK:(g + 1) * BLOCK], outs[1][g * BLOCK:(g + 1) * BLOCK])


def _win_bias():
    t = jnp.arange(BLOCK)[:, None]
    c = jnp.arange(3 * BLOCK + CTX_LEN)[None, :]
    piece = c // BLOCK
    offset = c - t - BLOCK
    window = (offset >= -BLOCK) & (offset <= BLOCK)
    ctx_keys = jnp.broadcast_to(piece >= 3, window.shape)
    kinds = [
        ctx_keys,
        ctx_keys | (window & (piece >= 1) & (piece < 3)),
        ctx_keys | (window & (piece < 3)),
        ctx_keys | (window & (piece < 2)),
    ]
    return jnp.where(jnp.stack(kinds), 0.0, NEG_INF).astype(F32)


def _win_attn(sink, q, k, v):
    bsz, t_len, _ = q.shape
    n_blocks = t_len // BLOCK
    n_ctx_blocks = CTX_LEN // BLOCK
    assert n_blocks - n_ctx_blocks >= 2
    kv = lambda fn: pl.BlockSpec((None, BLOCK, A_KV_WIDTH), fn)
    prev = lambda b, i: (b, jnp.maximum(i - 1, 0), 0)
    cur = lambda b, i: (b, i, 0)
    nxt = lambda b, i: (b, jnp.minimum(i + 1, n_blocks - 1), 0)
    ctx = pl.BlockSpec((None, CTX_LEN, A_KV_WIDTH), lambda b, i: (b, 0, 0))

    def kind(b, i):
        latent = jnp.where(i == n_ctx_blocks, 1, jnp.where(i == n_blocks - 1, 3, 2))
        return (jnp.where(i < n_ctx_blocks, 0, latent), 0, 0)

    return pl.pallas_call(
        _win_attn_kernel,
        grid=(bsz, n_blocks),
        in_specs=[
            pl.BlockSpec(memory_space=pltpu.SMEM),
            pl.BlockSpec((None, BLOCK, A_WIDTH), cur),
            kv(prev), kv(cur), kv(nxt), ctx,
            kv(prev), kv(cur), kv(nxt), ctx,
            pl.BlockSpec((None, BLOCK, 3 * BLOCK + CTX_LEN), kind),
        ],
        out_specs=pl.BlockSpec((None, BLOCK, A_WIDTH), cur),
        out_shape=jax.ShapeDtypeStruct((bsz, t_len, A_WIDTH), F32),
        compiler_params=_params(("parallel", "parallel")),
        name="window_attention",
    )(sink, q, k, k, k, k, v, v, v, v, _win_bias())


def _ssd_prep_kernel(xp_ref, xc_ref, xn_ref, dt_ref, cw_ref, cb_ref, dtb_ref,
                     xs_ref, c_ref, bt_ref, cbm_ref, dtsp_ref, ext_ref, *, n_chunks, n_ctx_chunks):
    chunk = pl.program_id(1)
    seq_first = (chunk == 0) | (chunk == n_ctx_chunks)
    seq_last = (chunk == n_ctx_chunks - 1) | (chunk == n_chunks - 1)
    ext_ref[0:SUBLANES, :] = xp_ref[...] * jnp.where(seq_first, 0.0, 1.0)
    ext_ref[SUBLANES:SUBLANES + B_CHUNK, :] = xc_ref[...]
    ext_ref[SUBLANES + B_CHUNK:2 * SUBLANES + B_CHUNK, :] = xn_ref[...] * jnp.where(seq_last, 0.0, 1.0)
    ext = ext_ref[...]
    n_ext = B_CHUNK + 2 * SUBLANES
    conv = cb_ref[...]
    for j in range(B_CONV):
        shifted = ext if j == B_CONV // 2 else pltpu.roll(ext, (B_CONV // 2 - j) % n_ext, 0)
        conv = conv + shifted[SUBLANES:SUBLANES + B_CHUNK, :] * cw_ref[j:j + 1, :]
    act = _silu(conv)
    xs_ref[...] = act[:, :B_INNER]
    for g in range(B_GROUPS):
        b_g = act[:, B_INNER + g * B_STATE:B_INNER + (g + 1) * B_STATE]
        c_g = act[:, B_INNER + (B_GROUPS + g) * B_STATE:B_INNER + (B_GROUPS + g + 1) * B_STATE].astype(BF16)
        c_ref[:, g * B_STATE:(g + 1) * B_STATE] = c_g
        bt_ref[g * B_STATE:(g + 1) * B_STATE, :] = b_g.T.astype(BF16)
        cbm_ref[g] = _dot_nt(c_g, b_g.astype(BF16))
    raw = dt_ref[...] + dtb_ref[...]
    dtsp_ref[...] = jnp.maximum(raw, 0.0) + jnp.log(1.0 + jnp.exp(-jnp.abs(raw)))


def _ssd_prep(xbc, dt, conv_w, conv_b, dt_bias):
    bsz, t_len, _ = xbc.shape
    n_chunks = t_len // B_CHUNK
    n_ctx_chunks = CTX_LEN // B_CHUNK
    rows8 = B_CHUNK // SUBLANES
    n_rows8 = t_len // SUBLANES
    halo = lambda fn: pl.BlockSpec((None, SUBLANES, B_XBC), fn)
    const = lambda shape: pl.BlockSpec(shape, lambda b, j: (0,) * len(shape))
    rows = lambda width: pl.BlockSpec((None, B_CHUNK, width), lambda b, j: (b, j, 0))
    return pl.pallas_call(
        functools.partial(_ssd_prep_kernel, n_chunks=n_chunks, n_ctx_chunks=n_ctx_chunks),
        grid=(bsz, n_chunks),
        in_specs=[
            halo(lambda b, j: (b, jnp.maximum(j * rows8 - 1, 0), 0)),
            rows(B_XBC),
            halo(lambda b, j: (b, jnp.minimum((j + 1) * rows8, n_rows8 - 1), 0)),
            rows(LANES),
            const((SUBLANES, B_XBC)), const((1, B_XBC)), const((1, LANES)),
        ],
        out_specs=[
            rows(B_INNER), rows(B_GROUPS * B_STATE),
            pl.BlockSpec((None, None, B_GROUPS * B_STATE, B_CHUNK), lambda b, j: (b, j, 0, 0)),
            pl.BlockSpec((None, None, B_GROUPS, B_CHUNK, B_CHUNK), lambda b, j: (b, j, 0, 0, 0)),
            rows(LANES),
        ],
        out_shape=[
            jax.ShapeDtypeStruct((bsz, t_len, B_INNER), F32),
            jax.ShapeDtypeStruct((bsz, t_len, B_GROUPS * B_STATE), BF16),
            jax.ShapeDtypeStruct((bsz, n_chunks, B_GROUPS * B_STATE, B_CHUNK), BF16),
            jax.ShapeDtypeStruct((bsz, n_chunks, B_GROUPS, B_CHUNK, B_CHUNK), F32),
            jax.ShapeDtypeStruct((bsz, t_len, LANES), F32),
        ],
        scratch_shapes=[pltpu.VMEM((B_CHUNK + 2 * SUBLANES, B_XBC), F32)],
        compiler_params=_params(("parallel", "parallel")),
        name="ssd_prep",
    )(xbc, xbc, xbc, dt, conv_w, conv_b, dt_bias)


def _ssd_direction(d, xs_ref, c_ref, bt_ref, cbm_ref, dt_ref, alog_ref, expand_ref, state_ref):
    backward = d == 1
    xs = xs_ref[...]
    lane = lax.broadcasted_iota(jnp.int32, (B_CHUNK, LANES), 1)
    row = lax.broadcasted_iota(jnp.int32, (B_CHUNK, LANES), 0)
    dir_lanes = (lane >= d * B_HEADS) & (lane < (d + 1) * B_HEADS)
    dt = dt_ref[...]
    a = -jnp.exp(alog_ref[...])
    dta = jnp.where(dir_lanes, dt * a, 0.0)
    tri_mask = (lane >= row) if backward else (lane <= row)
    tri = jnp.where(tri_mask, 1.0, 0.0).astype(BF16)
    tri_t = jnp.where((row >= lane) if backward else (row <= lane), 1.0, 0.0).astype(BF16)
    la_col = sum(_dot(tri, p) for p in _split3(dta))
    la_row = sum(_dot(p, tri_t) for p in _split3(dta.T))
    last = 0 if backward else B_CHUNK - 1
    ela = jnp.exp(la_col)
    w_end = jnp.exp(la_col[last:last + 1, :] - la_col)
    expand2 = expand_ref[d]

    def widen(v):
        hi, lo = _split2(jnp.where(dir_lanes, v, 0.0))
        return _dot(jnp.concatenate([hi, lo], axis=1), expand2)

    dt_e = widen(dt)
    ela_e = widen(ela)
    wdt_e = widen(w_end * dt)
    xdt = (xs * dt_e).astype(BF16)
    wx = (xs * wdt_e).astype(BF16)
    half_of_lane = lane // B_HEAD_DIM
    slabs = []
    gw = B_HPG * B_HEAD_DIM
    for g in range(B_GROUPS):
        c_g = c_ref[:, g * B_STATE:(g + 1) * B_STATE]
        cb = cbm_ref[g]
        h_in = state_ref[d, g]
        y_inter = _dot(c_g, h_in.astype(BF16)) * ela_e[:, g * gw:(g + 1) * gw]
        for pair in range(B_HPG // 2):
            col0 = g * gw + pair * LANES
            xdt_pair = xdt[:, col0:col0 + LANES]
            acc = y_inter[:, pair * LANES:(pair + 1) * LANES]
            for e in range(2):
                col = d * B_HEADS + g * B_HPG + 2 * pair + e
                seg = la_col[:, col:col + 1] - la_row[col:col + 1, :]
                decay = jnp.exp(jnp.where(tri_mask, seg, NEG_INF))
                m = (cb * decay).astype(BF16)
                rhs = jnp.where(half_of_lane == e, xdt_pair, jnp.zeros_like(xdt_pair))
                acc = acc + _dot(m, rhs)
            slabs.append(acc)
        chunk_decay = ela_e[last:last + 1, g * gw:(g + 1) * gw]
        state_ref[d, g] = h_in * chunk_decay + _dot(bt_ref[g * B_STATE:(g + 1) * B_STATE, :], wx[:, g * gw:(g + 1) * gw])
    return slabs, xs


def _ssd_kernel(xsf_ref, cf_ref, btf_ref, cbf_ref, dtf_ref, xsb_ref, cb_ref, btb_ref, cbb_ref, dtb_ref,
                alog_ref, dskip_ref, expand_ref, yf_ref, yb_ref, state_ref):
    @pl.when(pl.program_id(1) == 0)
    def _():
        state_ref[...] = jnp.zeros_like(state_ref)

    common = (alog_ref, expand_ref, state_ref)
    slabs, xs = _ssd_direction(0, xsf_ref, cf_ref, btf_ref, cbf_ref, dtf_ref, *common)
    for n, slab in enumerate(slabs):
        sl = slice(n * LANES, (n + 1) * LANES)
        yf_ref[:, sl] = slab + dskip_ref[:, sl] * xs[:, sl]
    slabs, _ = _ssd_direction(1, xsb_ref, cb_ref, btb_ref, cbb_ref, dtb_ref, *common)
    for n, slab in enumerate(slabs):
        yb_ref[:, n * LANES:(n + 1) * LANES] = slab


def _ssd(xs, c, bt, cbm, dt, a_log, d_skip_e, expand2):
    bsz, t_len, _ = xs.shape
    n_chunks = t_len // B_CHUNK
    n_ctx_chunks = CTX_LEN // B_CHUNK

    def chunk_b(j):
        return jnp.where(j < n_ctx_chunks, n_ctx_chunks - 1 - j, n_chunks + n_ctx_chunks - 1 - j)

    def specs(chunk_of):
        return [
            pl.BlockSpec((None, B_CHUNK, B_INNER), lambda b, j: (b, chunk_of(j), 0)),
            pl.BlockSpec((None, B_CHUNK, B_GROUPS * B_STATE), lambda b, j: (b, chunk_of(j), 0)),
            pl.BlockSpec((None, None, B_GROUPS * B_STATE, B_CHUNK), lambda b, j: (b, chunk_of(j), 0, 0)),
            pl.BlockSpec((None, None, B_GROUPS, B_CHUNK, B_CHUNK), lambda b, j: (b, chunk_of(j), 0, 0, 0)),
            pl.BlockSpec((None, B_CHUNK, LANES), lambda b, j: (b, chunk_of(j), 0)),
        ]

    const = lambda shape: pl.BlockSpec(shape, lambda b, j: (0,) * len(shape))
    y_spec = lambda chunk_of: pl.BlockSpec((None, B_CHUNK, B_INNER), lambda b, j: (b, chunk_of(j), 0))
    y_shape = jax.ShapeDtypeStruct((bsz, t_len, B_INNER), F32)
    return pl.pallas_call(
        _ssd_kernel,
        grid=(bsz, n_chunks),
        in_specs=specs(lambda j: j) + specs(chunk_b) + [
            const((1, LANES)), const((1, B_INNER)), const((2, 2 * LANES, B_INNER)),
        ],
        out_specs=[y_spec(lambda j: j), y_spec(chunk_b)],
        out_shape=[y_shape, y_shape],
        scratch_shapes=[pltpu.VMEM((2, B_GROUPS, B_STATE, B_HPG * B_HEAD_DIM), F32)],
        compiler_params=_params(("parallel", "arbitrary")),
        name="ssd_scan",
    )(xs, c, bt, cbm, dt, xs, c, bt, cbm, dt, a_log, d_skip_e, expand2)


def _outproj_even_kernel(att_ref, ga_ref, yf_ref, yb_ref, z_ref, ctx_ref, lat_ref, mod_ref, sn_ref, w_ref, o_ref):
    tok = jnp.where(pl.program_id(1) < CTX_LEN // ROW_TILE, ctx_ref[...], lat_ref[...])
    att = (att_ref[...] * _silu(ga_ref[...])).astype(BF16)
    y = (yf_ref[...] + yb_ref[...]) * _silu(z_ref[...])
    ms = jnp.mean(y * y, axis=-1, keepdims=True)
    yn = (y * lax.rsqrt(ms + NORM_EPS) * sn_ref[...]).astype(BF16)
    out = _dot(att, w_ref[:A_WIDTH, :]) + _dot(yn, w_ref[A_WIDTH:, :])
    o_ref[...] = tok + mod_ref[2:3, :] * out


def _outproj_even(att, ga, yf, yb, z, ctx, x, mods, ssm_norm, w):
    bsz, t_len, _ = att.shape
    ctx_tiles = CTX_LEN // ROW_TILE
    row = lambda width: pl.BlockSpec((None, ROW_TILE, width), lambda b, i: (b, i, 0))
    const = lambda shape: pl.BlockSpec(shape, lambda b, i: (0,) * len(shape))
    return pl.pallas_call(
        _outproj_even_kernel,
        grid=(bsz, t_len // ROW_TILE),
        in_specs=[
            row(A_WIDTH), row(A_WIDTH), row(B_INNER), row(B_INNER), row(B_INNER),
            pl.BlockSpec((None, ROW_TILE, D_MODEL), lambda b, i: (b, 0, 0)),
            pl.BlockSpec((None, ROW_TILE, D_MODEL), lambda b, i: (b, jnp.maximum(i - ctx_tiles, 0), 0)),
            pl.BlockSpec((None, None, SUBLANES, D_MODEL), lambda b, i: (b, jnp.minimum(i, 1), 0, 0)),
            const((1, B_INNER)), const(w.shape),
        ],
        out_specs=row(D_MODEL),
        out_shape=jax.ShapeDtypeStruct((bsz, t_len, D_MODEL), F32),
        compiler_params=_params(("parallel", "parallel")),
        name="outproj_even",
    )(att, ga, yf, yb, z, ctx, x, mods, ssm_norm.reshape(1, B_INNER), w)


ONES_ROWS = 16


def _diff_attn_kernel(lam_ref, qt_ref, k_ref, vt_ref, hn_ref, o_ref, s_ref, acc_ref, m_ref,
                      *, key_tile, lambda_init):
    tq = qt_ref.shape[1]
    n_tiles = k_ref.shape[0] // key_tile
    row = lax.broadcasted_iota(jnp.int32, (LANES, tq), 0)
    qt = qt_ref[...]
    zero = jnp.zeros_like(qt)
    qz = [jnp.where((row // HEAD_DIM) == c, qt, zero) for c in range(2)]
    ones = jnp.ones((ONES_ROWS, key_tile), BF16)

    def scores(t, slot):
        start = pl.multiple_of(t * key_tile, key_tile)
        k = k_ref[pl.ds(start, key_tile), :]
        for c in range(2):
            s_ref[slot, c] = _dot(k, qz[c])

    def consume(t, slot):
        start = pl.multiple_of(t * key_tile, key_tile)
        v1 = jnp.concatenate([vt_ref[:, pl.ds(start, key_tile)], ones], axis=0)
        for c in range(2):
            s = s_ref[slot, c]
            m_old = m_ref[c]
            m_new = jnp.maximum(m_old, jnp.max(s, axis=0, keepdims=True))
            alpha = jnp.exp2(m_old - m_new)
            p = jnp.exp2(s - m_new).astype(BF16)
            acc_ref[c] = alpha * acc_ref[c] + _dot(v1, p)
            m_ref[c] = m_new

    m_ref[...] = jnp.full(m_ref.shape, NEG_INF, F32)
    acc_ref[...] = jnp.zeros(acc_ref.shape, F32)
    scores(0, 0)

    def body(tt, carry):
        scores(2 * tt + 1, 1)
        consume(2 * tt, 0)
        scores(2 * tt + 2, 0)
        consume(2 * tt + 1, 1)
        return carry

    lax.fori_loop(0, (n_tiles - 1) // 2, body, 0)
    consume(n_tiles - 1, 0)

    lp = lam_ref[...]
    lam = (jnp.exp(jnp.sum(lp[0:1] * lp[1:2], axis=-1, keepdims=True))
           - jnp.exp(jnp.sum(lp[2:3] * lp[3:4], axis=-1, keepdims=True)) + lambda_init)
    a0 = acc_ref[0]
    a1 = acc_ref[1]
    o = a0[:LANES] * (1.0 / a0[LANES:LANES + 1]) - lam * (a1[:LANES] * (1.0 / a1[LANES:LANES + 1]))
    ms = jnp.mean(o * o, axis=0, keepdims=True)
    o = o * lax.rsqrt(ms + NORM_EPS) * hn_ref[...] * (1.0 - lambda_init)
    o_ref[...] = o.T


def _diff_attn(lam_params, qt, k, vt, head_norm, lambda_init):
    bsz, _, seq = qt.shape
    t_len = k.shape[1]
    q_tile = ATTN_Q_TILE
    key_tile = ATTN_KEY_TILE
    assert seq % q_tile == 0 and t_len % key_tile == 0 and (t_len // key_tile) % 2 == 1
    return pl.pallas_call(
        functools.partial(_diff_attn_kernel, key_tile=key_tile, lambda_init=lambda_init),
        grid=(bsz, C_HEADS, seq // q_tile),
        in_specs=[
            pl.BlockSpec(lam_params.shape, lambda b, h, i: (0, 0)),
            pl.BlockSpec((None, C_VDIM, q_tile), lambda b, h, i: (b, h, i)),
            pl.BlockSpec((None, t_len, C_VDIM), lambda b, h, i: (b, 0, h)),
            pl.BlockSpec((None, C_VDIM, t_len), lambda b, h, i: (b, h, 0)),
            pl.BlockSpec((C_VDIM, 1), lambda b, h, i: (0, 0)),
        ],
        out_specs=pl.BlockSpec((None, q_tile, C_VDIM), lambda b, h, i: (b, i, h)),
        out_shape=jax.ShapeDtypeStruct((bsz, seq, C_WIDTH), F32),
        scratch_shapes=[
            pltpu.VMEM((2, 2, key_tile, q_tile), F32),
            pltpu.VMEM((2, LANES + ONES_ROWS, q_tile), F32),
            pltpu.VMEM((2, 1, q_tile), F32),
        ],
        compiler_params=_params(("parallel", "parallel", "parallel")),
        name="diff_attention",
    )(lam_params, qt, k, vt, head_norm.reshape(C_VDIM, 1))


def _outproj_odd_kernel(o_ref_in, g_ref, tok_ref, mod_ref, w_ref, o_ref):
    mix = (o_ref_in[...] * _silu(g_ref[...])).astype(BF16)
    o_ref[...] = tok_ref[...] + mod_ref[2:3, :] * _dot(mix, w_ref[...])


def _outproj_odd(o, g, tok, mods, w):
    bsz, seq, _ = o.shape
    ctx_tiles = CTX_LEN // ROW_TILE
    lat = lambda width: pl.BlockSpec((None, ROW_TILE, width), lambda b, i: (b, i, 0))
    cat = lambda width: pl.BlockSpec((None, ROW_TILE, width), lambda b, i: (b, i + ctx_tiles, 0))
    return pl.pallas_call(
        _outproj_odd_kernel,
        grid=(bsz, seq // ROW_TILE),
        in_specs=[
            lat(C_WIDTH), cat(C_WIDTH), cat(D_MODEL),
            pl.BlockSpec((None, None, SUBLANES, D_MODEL), lambda b, i: (b, 1, 0, 0)),
            pl.BlockSpec(w.shape, lambda b, i: (0, 0)),
        ],
        out_specs=lat(D_MODEL),
        out_shape=jax.ShapeDtypeStruct((bsz, seq, D_MODEL), F32),
        compiler_params=_params(("parallel", "parallel")),
        name="outproj_odd",
    )(o, g, tok, mods, w)


def _rope_tables(seq):
    rows = seq // GRID_W
    row = jnp.repeat(jnp.arange(rows), GRID_W).astype(F32)
    col = jnp.tile(jnp.arange(GRID_W), rows).astype(F32)
    n_freq = HEAD_DIM // 4
    inv_freq = ROPE_BASE ** (-jnp.arange(n_freq, dtype=F32) / n_freq)
    ang_r = row[:, None] * inv_freq
    ang_c = col[:, None] * inv_freq
    ang = jnp.concatenate([ang_r, ang_r, ang_c, ang_c], axis=-1)
    sign = jnp.tile(jnp.concatenate([-jnp.ones((n_freq,), F32), jnp.ones((n_freq,), F32)]), 2)
    cos = jnp.concatenate([jnp.ones((CTX_LEN, HEAD_DIM), F32), jnp.cos(ang)], axis=0)
    sin = jnp.concatenate([jnp.zeros((CTX_LEN, HEAD_DIM), F32), jnp.sin(ang) * sign], axis=0)
    reps = LANES // HEAD_DIM
    return jnp.tile(cos, (1, reps)), jnp.tile(sin, (1, reps))


def _pad_lanes(v):
    return jnp.pad(v, ((0, 0), (0, LANES - v.shape[1])))


def kernel(x, c, ctx, c_ctx, mod_w, mod_b, norm_w, ev_w_in, ev_w_out, ev_q_norm, ev_k_norm, ev_sink,
           ev_conv_w, ev_conv_b, ev_dt_bias, ev_a_log, ev_d_skip, ev_ssm_norm, od_w_in, od_w_out,
           od_q_norm, od_k_norm, od_lambda, od_head_norm):
    assert mod_w.shape[0] == DEPTH == 2
    bsz, seq, _ = x.shape
    assert ctx.shape[1] == CTX_LEN and seq % ROW_TILE == 0 and CTX_LEN == ROW_TILE

    cc = jnp.concatenate([c, c_ctx[None], jnp.zeros((SUBLANES - bsz - 1, D_MODEL), F32)], axis=0)
    mod_all = _modulation(cc, mod_w, mod_b)

    def mod_table(li):
        m = mod_all[li].reshape(SUBLANES, 3, D_MODEL)
        rows = jnp.stack([jnp.broadcast_to(m[bsz], (bsz, 3, D_MODEL)), m[:bsz]], axis=1)
        return jnp.pad(rows, ((0, 0), (0, 0), (0, SUBLANES - 3), (0, 0)))

    cos, sin_signed = _rope_tables(seq)
    seg = jnp.arange(LANES) // HEAD_DIM
    seg_ones = (seg[:, None] == seg[None, :]).astype(BF16)
    seg_ones2 = jnp.concatenate([seg_ones, seg_ones], axis=0)
    tile_norm = lambda w: jnp.tile(w, LANES // HEAD_DIM).reshape(1, LANES)

    perm = jnp.array([(hh * A_GROUP + g) * HEAD_DIM + e
                      for g in range(A_GROUP) for hh in range(A_KV_HEADS) for e in range(HEAD_DIM)])
    w_in = ev_w_in[0]
    o_q, o_k, o_v, o_ga, o_z, o_xbc, o_dt = [int(s) for s in
        [0, A_WIDTH, A_WIDTH + A_KV_WIDTH, A_WIDTH + 2 * A_KV_WIDTH, 2 * A_WIDTH + 2 * A_KV_WIDTH,
         2 * A_WIDTH + 2 * A_KV_WIDTH + B_INNER, 2 * A_WIDTH + 2 * A_KV_WIDTH + B_INNER + B_XBC]]
    w0 = jnp.concatenate([
        w_in[:, o_q:o_k][:, perm], w_in[:, o_k:o_ga], w_in[:, o_ga:o_z][:, perm], w_in[:, o_z:],
        jnp.zeros((D_MODEL, LANES - 2 * B_HEADS), F32)], axis=1).astype(BF16)
    q_scale = HEAD_DIM ** -0.5 * LOG2E
    outs0 = (_Out(o_q, A_WIDTH, BF16, "q", q_scale), _Out(o_k, A_KV_WIDTH, BF16, "k"),
             _Out(o_v, A_KV_WIDTH, BF16), _Out(o_ga, A_WIDTH, F32), _Out(o_z, B_INNER, F32),
             _Out(o_xbc, B_XBC, F32), _Out(o_dt, LANES, F32))
    mods0 = mod_table(0)
    q, k, v, ga, z, xbc, dt = _inproj(
        ctx, x, mods0, norm_w[0], w0, cos, sin_signed, tile_norm(ev_q_norm[0]), tile_norm(ev_k_norm[0]),
        seg_ones2, outs0)
    att = _win_attn(ev_sink[0], q, k, v)
    expand = jnp.stack([(jnp.arange(LANES)[:, None] == d * B_HEADS + jnp.arange(B_INNER)[None, :] // B_HEAD_DIM)
                        for d in range(2)]).astype(BF16)
    expand2 = jnp.concatenate([expand, expand], axis=1)
    conv_w = jnp.pad(ev_conv_w[0], ((0, SUBLANES - B_CONV), (0, 0)))
    xs, c_bf, b_t, c_bt, dt_sp = _ssd_prep(xbc, dt, conv_w, ev_conv_b[0].reshape(1, B_XBC),
                                           _pad_lanes(ev_dt_bias[0].reshape(1, 2 * B_HEADS)))
    yf, yb = _ssd(xs, c_bf, b_t, c_bt, dt_sp, _pad_lanes(ev_a_log[0].reshape(1, 2 * B_HEADS)),
                  jnp.repeat(ev_d_skip[0], B_HEAD_DIM).reshape(1, B_INNER), expand2)
    w_out0 = jnp.concatenate([ev_w_out[0][:A_WIDTH][perm], ev_w_out[0][A_WIDTH:]], axis=0).astype(BF16)
    tok = _outproj_even(att, ga, yf, yb, z, ctx, x, mods0, ev_ssm_norm[0], w_out0)

    lambda_init = 0.8 - 0.6 * math.exp(-0.3 * 1)
    outs1 = (_Out(0, C_WIDTH, BF16, "q", q_scale, latent_only=True, transposed=True),
             _Out(C_WIDTH, C_WIDTH, BF16, "k"), _Out(2 * C_WIDTH, C_WIDTH, BF16, transposed=True),
             _Out(3 * C_WIDTH, C_WIDTH, F32))
    mods1 = mod_table(1)
    qt, k, vt, g = _inproj(
        tok, tok, mods1, norm_w[1], od_w_in[0].astype(BF16), cos, sin_signed, tile_norm(od_q_norm[0]),
        tile_norm(od_k_norm[0]), seg_ones2, outs1)
    o = _diff_attn(od_lambda[0], qt, k, vt, od_head_norm[0], lambda_init)
    return _outproj_odd(o, g, tok, mods1, od_w_out[0].astype(BF16))
```

```python
import functools
import math
from typing import NamedTuple

import jax
import jax.numpy as jnp
from jax import lax
from jax.experimental import pallas as pl
from jax.experimental.pallas import tpu as pltpu

D_MODEL = 1024
DEPTH = 2
GRID_W = 64
CTX_LEN = 256
HEAD_DIM = 64
ROPE_BASE = 10000.0
NORM_EPS = 1e-6
BLOCK = 128

A_HEADS = 8
A_KV_HEADS = 2
A_GROUP = A_HEADS // A_KV_HEADS
A_WIDTH = A_HEADS * HEAD_DIM
A_KV_WIDTH = A_KV_HEADS * HEAD_DIM

B_INNER = D_MODEL
B_HEAD_DIM = 64
B_HEADS = B_INNER // B_HEAD_DIM
B_GROUPS = 2
B_HPG = B_HEADS // B_GROUPS
B_STATE = 128
B_CONV = 5
B_CHUNK = 128
B_XBC = B_INNER + 2 * B_GROUPS * B_STATE

C_HEADS = D_MODEL // (2 * HEAD_DIM)
C_VDIM = 2 * HEAD_DIM
C_WIDTH = C_HEADS * C_VDIM

LANES = 128
SUBLANES = 8
ROW_TILE = 256
ATTN_Q_TILE = 2048
ATTN_KEY_TILE = 768
ATTN_COL_BLOCK = 256
VMEM_LIMIT = 56 * 1024 * 1024

F32 = jnp.float32
BF16 = jnp.bfloat16
NEG_INF = float("-inf")
LOG2E = math.log2(math.e)


def _silu(x):
    return x * (1.0 / (1.0 + jnp.exp(-x)))


def _dot(a, b):
    return jnp.dot(a, b, preferred_element_type=F32)


def _dot_nt(a, b):
    return lax.dot_general(a, b, (((1,), (1,)), ((), ())), preferred_element_type=F32)


def _split2(x):
    hi = x.astype(BF16)
    lo = (x - hi.astype(F32)).astype(BF16)
    return hi, lo


def _split3(x):
    hi = x.astype(BF16)
    r = x - hi.astype(F32)
    mid = r.astype(BF16)
    lo = (r - mid.astype(F32)).astype(BF16)
    return hi, mid, lo


def _params(sem):
    return pltpu.CompilerParams(dimension_semantics=sem, vmem_limit_bytes=VMEM_LIMIT)


def _mod_kernel(cc_ref, w_ref, b_ref, o_ref):
    s = _silu(cc_ref[...])
    o_ref[...] = jnp.dot(s, w_ref[...], precision=lax.Precision.HIGHEST,
                         preferred_element_type=F32) + b_ref[...]


def _modulation(cc, mod_w, mod_b):
    depth = mod_w.shape[0]
    n_tiles = 3
    return pl.pallas_call(
        _mod_kernel,
        grid=(depth, n_tiles),
        in_specs=[
            pl.BlockSpec((SUBLANES, D_MODEL), lambda l, n: (0, 0)),
            pl.BlockSpec((None, D_MODEL, D_MODEL), lambda l, n: (l, 0, n)),
            pl.BlockSpec((None, 1, D_MODEL), lambda l, n: (l, 0, n)),
        ],
        out_specs=pl.BlockSpec((None, SUBLANES, D_MODEL), lambda l, n: (l, 0, n)),
        out_shape=jax.ShapeDtypeStruct((depth, SUBLANES, 3 * D_MODEL), F32),
        compiler_params=_params(("arbitrary", "arbitrary")),
        name="modulation",
    )(cc, mod_w, mod_b.reshape(depth, 1, 3 * D_MODEL))


class _Out(NamedTuple):
    start: int
    width: int
    dtype: object
    kind: str = "plain"
    scale: float = 1.0
    latent_only: bool = False
    transposed: bool = False


def _head_norm_rope(a, nw, cos, sin_signed, seg_ones2, first_quarter):
    hi, lo = _split2(a * a)
    ssq = _dot(jnp.concatenate([hi, lo], axis=1), seg_ones2)
    n = a * lax.rsqrt(ssq * (1.0 / HEAD_DIM) + NORM_EPS) * nw
    quarter = HEAD_DIM // 4
    rot = jnp.where(first_quarter, pltpu.roll(n, LANES - quarter, 1), pltpu.roll(n, quarter, 1))
    return n * cos + rot * sin_signed


def _inproj_kernel(ctx_ref, lat_ref, mod_ref, nw_ref, w_ref, cos_ref, sin_ref, qn_ref, kn_ref, ones_ref,
                   *out_refs, outs):
    x = jnp.where(pl.program_id(1) < CTX_LEN // ROW_TILE, ctx_ref[...], lat_ref[...])
    ms = jnp.mean(x * x, axis=-1, keepdims=True)
    y = x * lax.rsqrt(ms + NORM_EPS) * nw_ref[...]
    h = y * (1.0 + mod_ref[1:2, :]) + mod_ref[0:1, :]
    hb = h.astype(BF16)
    lane = lax.broadcasted_iota(jnp.int32, (x.shape[0], LANES), 1)
    first_quarter = (lane % (HEAD_DIM // 2)) < (HEAD_DIM // 4)
    for out, o_ref in zip(outs, out_refs):
        wide = _dot(hb, w_ref[:, out.start:out.start + out.width])
        if out.kind == "plain" and not out.transposed:
            o_ref[...] = wide.astype(out.dtype)
            continue
        for s in range(out.width // LANES):
            cols = slice(s * LANES, (s + 1) * LANES)
            r = wide[:, cols]
            if out.kind != "plain":
                nw = qn_ref[...] if out.kind == "q" else kn_ref[...]
                r = _head_norm_rope(r, nw, cos_ref[...], sin_ref[...], ones_ref[...], first_quarter) * out.scale
            if out.transposed:
                o_ref[cols, :] = r.astype(out.dtype).T
            else:
                o_ref[:, cols] = r.astype(out.dtype)


def _inproj(ctx_src, lat_src, mods, norm_w, w, cos, sin_signed, q_norm, k_norm, seg_ones2, outs):
    bsz = lat_src.shape[0]
    ctx_tiles = CTX_LEN // ROW_TILE
    split = ctx_src is not lat_src
    t_len = CTX_LEN + lat_src.shape[1] if split else lat_src.shape[1]
    n_row = t_len // ROW_TILE
    lat_row = (lambda i: jnp.maximum(i - ctx_tiles, 0)) if split else (lambda i: i)
    const = lambda shape: pl.BlockSpec(shape, lambda b, i: (0,) * len(shape))

    def out_spec(out):
        r = (lambda i: jnp.maximum(i - ctx_tiles, 0)) if out.latent_only else (lambda i: i)
        if out.transposed:
            return pl.BlockSpec((None, out.width, ROW_TILE), lambda b, i: (b, 0, r(i)))
        return pl.BlockSpec((None, ROW_TILE, out.width), lambda b, i: (b, r(i), 0))

    def out_shape(out):
        rows = t_len - CTX_LEN if out.latent_only else t_len
        return jax.ShapeDtypeStruct((bsz, out.width, rows) if out.transposed else (bsz, rows, out.width), out.dtype)

    return pl.pallas_call(
        functools.partial(_inproj_kernel, outs=outs),
        grid=(bsz, n_row),
        in_specs=[
            pl.BlockSpec((None, ROW_TILE, D_MODEL), lambda b, i: (b, 0, 0)),
            pl.BlockSpec((None, ROW_TILE, D_MODEL), lambda b, i: (b, lat_row(i), 0)),
            pl.BlockSpec((None, None, SUBLANES, D_MODEL), lambda b, i: (b, jnp.minimum(i, 1), 0, 0)),
            const((1, D_MODEL)),
            const(w.shape),
            pl.BlockSpec((ROW_TILE, LANES), lambda b, i: (i, 0)),
            pl.BlockSpec((ROW_TILE, LANES), lambda b, i: (i, 0)),
            const((1, LANES)),
            const((1, LANES)),
            const((2 * LANES, LANES)),
        ],
        out_specs=[out_spec(o) for o in outs],
        out_shape=[out_shape(o) for o in outs],
        compiler_params=_params(("parallel", "arbitrary")),
        name="inproj",
    )(ctx_src, lat_src, mods, norm_w.reshape(1, D_MODEL), w, cos, sin_signed, q_norm, k_norm, seg_ones2)


def _win_attn_kernel(sink_ref, q_ref, kp_ref, kc_ref, kn_ref, kx_ref, vp_ref, vc_ref, vn_ref, vx_ref,
                     bias_ref, o_ref):
    kk = jnp.concatenate([kp_ref[...], kc_ref[...], kn_ref[...], kx_ref[...]], axis=0)
    vv = jnp.concatenate([vp_ref[...], vc_ref[...], vn_ref[...], vx_ref[...]], axis=0)
    n_keys = kk.shape[0]
    v1 = jnp.concatenate([vv, jnp.ones((n_keys, LANES), BF16)], axis=1)
    rows = A_GROUP * BLOCK
    bias = bias_ref[...]
    lane = lax.broadcasted_iota(jnp.int32, (BLOCK, LANES), 1)
    head_of_lane = lane // HEAD_DIM
    g_of_row = lax.broadcasted_iota(jnp.int32, (rows, 1), 0) // BLOCK
    outs = []
    for hh in range(A_KV_HEADS):
        qs = jnp.concatenate(
            [jnp.where(head_of_lane == hh, q_ref[:, g * LANES:(g + 1) * LANES], jnp.zeros((BLOCK, LANES), BF16))
             for g in range(A_GROUP)], axis=0)
        s = _dot_nt(qs, kk).reshape(A_GROUP, BLOCK, n_keys) + bias[None]
        s = s.reshape(rows, n_keys)
        sink = jnp.zeros((rows, 1), F32)
        for g in range(A_GROUP):
            sink = jnp.where(g_of_row == g, sink_ref[hh * A_GROUP + g] * LOG2E, sink)
        m = jnp.maximum(jnp.max(s, axis=-1, keepdims=True), sink)
        e = jnp.exp2(s - m).astype(BF16)
        ov = _dot(e, v1)
        den = jnp.exp2(sink - m) + ov[:, LANES:]
        outs.append(ov[:, :LANES] * (1.0 / den))
    for g in range(A_GROUP):
        o_ref[:, g * LANES:(g + 1) * LANES] = jnp.where(
            head_of_lane == 0, outs[0][g * BLOCK:(g + 1) * BLOCK], outs[1][g * BLOCK:(g + 1) * BLOCK])


def _win_bias():
    t = jnp.arange(BLOCK)[:, None]
    c = jnp.arange(3 * BLOCK + CTX_LEN)[None, :]
    piece = c // BLOCK
    offset = c - t - BLOCK
    window = (offset >= -BLOCK) & (offset <= BLOCK)
    ctx_keys = jnp.broadcast_to(piece >= 3, window.shape)
    kinds = [
        ctx_keys,
        ctx_keys | (window & (piece >= 1) & (piece < 3)),
        ctx_keys | (window & (piece < 3)),
        ctx_keys | (window & (piece < 2)),
    ]
    return jnp.where(jnp.stack(kinds), 0.0, NEG_INF).astype(F32)


def _win_attn(sink, q, k, v):
    bsz, t_len, _ = q.shape
    n_blocks = t_len // BLOCK
    n_ctx_blocks = CTX_LEN // BLOCK
    assert n_blocks - n_ctx_blocks >= 2
    kv = lambda fn: pl.BlockSpec((None, BLOCK, A_KV_WIDTH), fn)
    prev = lambda b, i: (b, jnp.maximum(i - 1, 0), 0)
    cur = lambda b, i: (b, i, 0)
    nxt = lambda b, i: (b, jnp.minimum(i + 1, n_blocks - 1), 0)
    ctx = pl.BlockSpec((None, CTX_LEN, A_KV_WIDTH), lambda b, i: (b, 0, 0))

    def kind(b, i):
        latent = jnp.where(i == n_ctx_blocks, 1, jnp.where(i == n_blocks - 1, 3, 2))
        return (jnp.where(i < n_ctx_blocks, 0, latent), 0, 0)

    return pl.pallas_call(
        _win_attn_kernel,
        grid=(bsz, n_blocks),
        in_specs=[
            pl.BlockSpec(memory_space=pltpu.SMEM),
            pl.BlockSpec((None, BLOCK, A_WIDTH), cur),
            kv(prev), kv(cur), kv(nxt), ctx,
            kv(prev), kv(cur), kv(nxt), ctx,
            pl.BlockSpec((None, BLOCK, 3 * BLOCK + CTX_LEN), kind),
        ],
        out_specs=pl.BlockSpec((None, BLOCK, A_WIDTH), cur),
        out_shape=jax.ShapeDtypeStruct((bsz, t_len, A_WIDTH), F32),
        compiler_params=_params(("parallel", "parallel")),
        name="window_attention",
    )(sink, q, k, k, k, k, v, v, v, v, _win_bias())


def _ssd_prep_kernel(xp_ref, xc_ref, xn_ref, dt_ref, cw_ref, cb_ref, dtb_ref,
                     xs_ref, c_ref, bt_ref, cbm_ref, dtsp_ref, ext_ref, *, n_chunks, n_ctx_chunks):
    chunk = pl.program_id(1)
    seq_first = (chunk == 0) | (chunk == n_ctx_chunks)
    seq_last = (chunk == n_ctx_chunks - 1) | (chunk == n_chunks - 1)
    ext_ref[0:SUBLANES, :] = xp_ref[...] * jnp.where(seq_first, 0.0, 1.0)
    ext_ref[SUBLANES:SUBLANES + B_CHUNK, :] = xc_ref[...]
    ext_ref[SUBLANES + B_CHUNK:2 * SUBLANES + B_CHUNK, :] = xn_ref[...] * jnp.where(seq_last, 0.0, 1.0)
    ext = ext_ref[...]
    n_ext = B_CHUNK + 2 * SUBLANES
    conv = cb_ref[...]
    for j in range(B_CONV):
        shifted = ext if j == B_CONV // 2 else pltpu.roll(ext, (B_CONV // 2 - j) % n_ext, 0)
        conv = conv + shifted[SUBLANES:SUBLANES + B_CHUNK, :] * cw_ref[j:j + 1, :]
    act = _silu(conv)
    xs_ref[...] = act[:, :B_INNER]
    for g in range(B_GROUPS):
        b_g = act[:, B_INNER + g * B_STATE:B_INNER + (g + 1) * B_STATE]
        c_g = act[:, B_INNER + (B_GROUPS + g) * B_STATE:B_INNER + (B_GROUPS + g + 1) * B_STATE].astype(BF16)
        c_ref[:, g * B_STATE:(g + 1) * B_STATE] = c_g
        bt_ref[g * B_STATE:(g + 1) * B_STATE, :] = b_g.T.astype(BF16)
        cbm_ref[g] = _dot_nt(c_g, b_g.astype(BF16))
    raw = dt_ref[...] + dtb_ref[...]
    dtsp_ref[...] = jnp.maximum(raw, 0.0) + jnp.log(1.0 + jnp.exp(-jnp.abs(raw)))


def _ssd_prep(xbc, dt, conv_w, conv_b, dt_bias):
    bsz, t_len, _ = xbc.shape
    n_chunks = t_len // B_CHUNK
    n_ctx_chunks = CTX_LEN // B_CHUNK
    rows8 = B_CHUNK // SUBLANES
    n_rows8 = t_len // SUBLANES
    halo = lambda fn: pl.BlockSpec((None, SUBLANES, B_XBC), fn)
    const = lambda shape: pl.BlockSpec(shape, lambda b, j: (0,) * len(shape))
    rows = lambda width: pl.BlockSpec((None, B_CHUNK, width), lambda b, j: (b, j, 0))
    return pl.pallas_call(
        functools.partial(_ssd_prep_kernel, n_chunks=n_chunks, n_ctx_chunks=n_ctx_chunks),
        grid=(bsz, n_chunks),
        in_specs=[
            halo(lambda b, j: (b, jnp.maximum(j * rows8 - 1, 0), 0)),
            rows(B_XBC),
            halo(lambda b, j: (b, jnp.minimum((j + 1) * rows8, n_rows8 - 1), 0)),
            rows(LANES),
            const((SUBLANES, B_XBC)), const((1, B_XBC)), const((1, LANES)),
        ],
        out_specs=[
            rows(B_INNER), rows(B_GROUPS * B_STATE),
            pl.BlockSpec((None, None, B_GROUPS * B_STATE, B_CHUNK), lambda b, j: (b, j, 0, 0)),
            pl.BlockSpec((None, None, B_GROUPS, B_CHUNK, B_CHUNK), lambda b, j: (b, j, 0, 0, 0)),
            rows(LANES),
        ],
        out_shape=[
            jax.ShapeDtypeStruct((bsz, t_len, B_INNER), F32),
            jax.ShapeDtypeStruct((bsz, t_len, B_GROUPS * B_STATE), BF16),
            jax.ShapeDtypeStruct((bsz, n_chunks, B_GROUPS * B_STATE, B_CHUNK), BF16),
            jax.ShapeDtypeStruct((bsz, n_chunks, B_GROUPS, B_CHUNK, B_CHUNK), F32),
            jax.ShapeDtypeStruct((bsz, t_len, LANES), F32),
        ],
        scratch_shapes=[pltpu.VMEM((B_CHUNK + 2 * SUBLANES, B_XBC), F32)],
        compiler_params=_params(("parallel", "parallel")),
        name="ssd_prep",
    )(xbc, xbc, xbc, dt, conv_w, conv_b, dt_bias)


def _ssd_direction(d, xs_ref, c_ref, bt_ref, cbm_ref, dt_ref, alog_ref, expand_ref, state_ref):
    backward = d == 1
    xs = xs_ref[...]
    lane = lax.broadcasted_iota(jnp.int32, (B_CHUNK, LANES), 1)
    row = lax.broadcasted_iota(jnp.int32, (B_CHUNK, LANES), 0)
    dir_lanes = (lane >= d * B_HEADS) & (lane < (d + 1) * B_HEADS)
    dt = dt_ref[...]
    a = -jnp.exp(alog_ref[...])
    dta = jnp.where(dir_lanes, dt * a, 0.0)
    tri_mask = (lane >= row) if backward else (lane <= row)
    tri = jnp.where(tri_mask, 1.0, 0.0).astype(BF16)
    tri_t = jnp.where((row >= lane) if backward else (row <= lane), 1.0, 0.0).astype(BF16)
    la_col = sum(_dot(tri, p) for p in _split3(dta))
    la_row = sum(_dot(p, tri_t) for p in _split3(dta.T))
    last = 0 if backward else B_CHUNK - 1
    ela = jnp.exp(la_col)
    w_end = jnp.exp(la_col[last:last + 1, :] - la_col)
    expand2 = expand_ref[d]

    def widen(v):
        hi, lo = _split2(jnp.where(dir_lanes, v, 0.0))
        return _dot(jnp.concatenate([hi, lo], axis=1), expand2)

    dt_e = widen(dt)
    ela_e = widen(ela)
    wdt_e = widen(w_end * dt)
    xdt = (xs * dt_e).astype(BF16)
    wx = (xs * wdt_e).astype(BF16)
    half_of_lane = lane // B_HEAD_DIM
    slabs = []
    gw = B_HPG * B_HEAD_DIM
    for g in range(B_GROUPS):
        c_g = c_ref[:, g * B_STATE:(g + 1) * B_STATE]
        cb = cbm_ref[g]
        h_in = state_ref[d, g]
        y_inter = _dot(c_g, h_in.astype(BF16)) * ela_e[:, g * gw:(g + 1) * gw]
        for pair in range(B_HPG // 2):
            col0 = g * gw + pair * LANES
            xdt_pair = xdt[:, col0:col0 + LANES]
            acc = y_inter[:, pair * LANES:(pair + 1) * LANES]
            for e in range(2):
                col = d * B_HEADS + g * B_HPG + 2 * pair + e
                seg = la_col[:, col:col + 1] - la_row[col:col + 1, :]
                decay = jnp.exp(jnp.where(tri_mask, seg, NEG_INF))
                m = (cb * decay).astype(BF16)
                rhs = jnp.where(half_of_lane == e, xdt_pair, jnp.zeros_like(xdt_pair))
                acc = acc + _dot(m, rhs)
            slabs.append(acc)
        chunk_decay = ela_e[last:last + 1, g * gw:(g + 1) * gw]
        state_ref[d, g] = h_in * chunk_decay + _dot(bt_ref[g * B_STATE:(g + 1) * B_STATE, :], wx[:, g * gw:(g + 1) * gw])
    return slabs, xs


def _ssd_kernel(xsf_ref, cf_ref, btf_ref, cbf_ref, dtf_ref, xsb_ref, cb_ref, btb_ref, cbb_ref, dtb_ref,
                alog_ref, dskip_ref, expand_ref, yf_ref, yb_ref, state_ref):
    @pl.when(pl.program_id(1) == 0)
    def _():
        state_ref[...] = jnp.zeros_like(state_ref)

    common = (alog_ref, expand_ref, state_ref)
    slabs, xs = _ssd_direction(0, xsf_ref, cf_ref, btf_ref, cbf_ref, dtf_ref, *common)
    for n, slab in enumerate(slabs):
        sl = slice(n * LANES, (n + 1) * LANES)
        yf_ref[:, sl] = slab + dskip_ref[:, sl] * xs[:, sl]
    slabs, _ = _ssd_direction(1, xsb_ref, cb_ref, btb_ref, cbb_ref, dtb_ref, *common)
    for n, slab in enumerate(slabs):
        yb_ref[:, n * LANES:(n + 1) * LANES] = slab


def _ssd(xs, c, bt, cbm, dt, a_log, d_skip_e, expand2):
    bsz, t_len, _ = xs.shape
    n_chunks = t_len // B_CHUNK
    n_ctx_chunks = CTX_LEN // B_CHUNK

    def chunk_b(j):
        return jnp.where(j < n_ctx_chunks, n_ctx_chunks - 1 - j, n_chunks + n_ctx_chunks - 1 - j)

    def specs(chunk_of):
        return [
            pl.BlockSpec((None, B_CHUNK, B_INNER), lambda b, j: (b, chunk_of(j), 0)),
            pl.BlockSpec((None, B_CHUNK, B_GROUPS * B_STATE), lambda b, j: (b, chunk_of(j), 0)),
            pl.BlockSpec((None, None, B_GROUPS * B_STATE, B_CHUNK), lambda b, j: (b, chunk_of(j), 0, 0)),
            pl.BlockSpec((None, None, B_GROUPS, B_CHUNK, B_CHUNK), lambda b, j: (b, chunk_of(j), 0, 0, 0)),
            pl.BlockSpec((None, B_CHUNK, LANES), lambda b, j: (b, chunk_of(j), 0)),
        ]

    const = lambda shape: pl.BlockSpec(shape, lambda b, j: (0,) * len(shape))
    y_spec = lambda chunk_of: pl.BlockSpec((None, B_CHUNK, B_INNER), lambda b, j: (b, chunk_of(j), 0))
    y_shape = jax.ShapeDtypeStruct((bsz, t_len, B_INNER), F32)
    return pl.pallas_call(
        _ssd_kernel,
        grid=(bsz, n_chunks),
        in_specs=specs(lambda j: j) + specs(chunk_b) + [
            const((1, LANES)), const((1, B_INNER)), const((2, 2 * LANES, B_INNER)),
        ],
        out_specs=[y_spec(lambda j: j), y_spec(chunk_b)],
        out_shape=[y_shape, y_shape],
        scratch_shapes=[pltpu.VMEM((2, B_GROUPS, B_STATE, B_HPG * B_HEAD_DIM), F32)],
        compiler_params=_params(("parallel", "arbitrary")),
        name="ssd_scan",
    )(xs, c, bt, cbm, dt, xs, c, bt, cbm, dt, a_log, d_skip_e, expand2)


def _outproj_even_kernel(att_ref, ga_ref, yf_ref, yb_ref, z_ref, ctx_ref, lat_ref, mod_ref, sn_ref, w_ref, o_ref):
    tok = jnp.where(pl.program_id(1) < CTX_LEN // ROW_TILE, ctx_ref[...], lat_ref[...])
    att = (att_ref[...] * _silu(ga_ref[...])).astype(BF16)
    y = (yf_ref[...] + yb_ref[...]) * _silu(z_ref[...])
    ms = jnp.mean(y * y, axis=-1, keepdims=True)
    yn = (y * lax.rsqrt(ms + NORM_EPS) * sn_ref[...]).astype(BF16)
    out = _dot(att, w_ref[:A_WIDTH, :]) + _dot(yn, w_ref[A_WIDTH:, :])
    o_ref[...] = tok + mod_ref[2:3, :] * out


def _outproj_even(att, ga, yf, yb, z, ctx, x, mods, ssm_norm, w):
    bsz, t_len, _ = att.shape
    ctx_tiles = CTX_LEN // ROW_TILE
    row = lambda width: pl.BlockSpec((None, ROW_TILE, width), lambda b, i: (b, i, 0))
    const = lambda shape: pl.BlockSpec(shape, lambda b, i: (0,) * len(shape))
    return pl.pallas_call(
        _outproj_even_kernel,
        grid=(bsz, t_len // ROW_TILE),
        in_specs=[
            row(A_WIDTH), row(A_WIDTH), row(B_INNER), row(B_INNER), row(B_INNER),
            pl.BlockSpec((None, ROW_TILE, D_MODEL), lambda b, i: (b, 0, 0)),
            pl.BlockSpec((None, ROW_TILE, D_MODEL), lambda b, i: (b, jnp.maximum(i - ctx_tiles, 0), 0)),
            pl.BlockSpec((None, None, SUBLANES, D_MODEL), lambda b, i: (b, jnp.minimum(i, 1), 0, 0)),
            const((1, B_INNER)), const(w.shape),
        ],
        out_specs=row(D_MODEL),
        out_shape=jax.ShapeDtypeStruct((bsz, t_len, D_MODEL), F32),
        compiler_params=_params(("parallel", "parallel")),
        name="outproj_even",
    )(att, ga, yf, yb, z, ctx, x, mods, ssm_norm.reshape(1, B_INNER), w)


ONES_ROWS = 16


def _diff_attn_kernel(lam_ref, qt_ref, k_ref, vt_ref, hn_ref, o_ref, s_ref, mx_ref, acc_ref, m_ref,
                      *, key_tile, lambda_init):
    tq = qt_ref.shape[1]
    n_tiles = k_ref.shape[0] // key_tile
    n_cols = tq // ATTN_COL_BLOCK
    row = lax.broadcasted_iota(jnp.int32, (LANES, tq), 0)
    qt = qt_ref[...]
    zero = jnp.zeros_like(qt)
    qz = [jnp.where((row // HEAD_DIM) == c, qt, zero) for c in range(2)]
    ones = jnp.ones((ONES_ROWS, key_tile), BF16)

    def scores(t, slot, n):
        cols = slice(n * ATTN_COL_BLOCK, (n + 1) * ATTN_COL_BLOCK)
        start = pl.multiple_of(t * key_tile, key_tile)
        k = k_ref[pl.ds(start, key_tile), :]
        for c in range(2):
            s = _dot(k, qz[c][:, cols])
            s_ref[slot, c, :, cols] = s
            mx_ref[slot, c, :, cols] = jnp.max(s, axis=0, keepdims=True)

    def consume(t, slot, n):
        cols = slice(n * ATTN_COL_BLOCK, (n + 1) * ATTN_COL_BLOCK)
        start = pl.multiple_of(t * key_tile, key_tile)
        v1 = jnp.concatenate([vt_ref[:, pl.ds(start, key_tile)], ones], axis=0)
        for c in range(2):
            s = s_ref[slot, c, :, cols]
            m_old = m_ref[c, :, cols]
            m_new = jnp.maximum(m_old, mx_ref[slot, c, :, cols])
            alpha = jnp.exp2(m_old - m_new)
            p = jnp.exp2(s - m_new).astype(BF16)
            acc_ref[c, :, cols] = alpha * acc_ref[c, :, cols] + _dot(v1, p)
            m_ref[c, :, cols] = m_new

    m_ref[...] = jnp.full(m_ref.shape, NEG_INF, F32)
    acc_ref[...] = jnp.zeros(acc_ref.shape, F32)
    for n in range(n_cols):
        scores(0, 0, n)

    def body(tt, carry):
        for n in range(n_cols):
            scores(2 * tt + 1, 1, n)
            consume(2 * tt, 0, n)
        for n in range(n_cols):
            scores(2 * tt + 2, 0, n)
            consume(2 * tt + 1, 1, n)
        return carry

    lax.fori_loop(0, (n_tiles - 1) // 2, body, 0)
    for n in range(n_cols):
        consume(n_tiles - 1, 0, n)

    lp = lam_ref[...]
    lam = (jnp.exp(jnp.sum(lp[0:1] * lp[1:2], axis=-1, keepdims=True))
           - jnp.exp(jnp.sum(lp[2:3] * lp[3:4], axis=-1, keepdims=True)) + lambda_init)
    a0 = acc_ref[0]
    a1 = acc_ref[1]
    o = a0[:LANES] * (1.0 / a0[LANES:LANES + 1]) - lam * (a1[:LANES] * (1.0 / a1[LANES:LANES + 1]))
    ms = jnp.mean(o * o, axis=0, keepdims=True)
    o = o * lax.rsqrt(ms + NORM_EPS) * hn_ref[...] * (1.0 - lambda_init)
    o_ref[...] = o.T


def _diff_attn(lam_params, qt, k, vt, head_norm, lambda_init):
    bsz, _, seq = qt.shape
    t_len = k.shape[1]
    q_tile = ATTN_Q_TILE
    key_tile = ATTN_KEY_TILE
    assert seq % q_tile == 0 and t_len % key_tile == 0 and (t_len // key_tile) % 2 == 1
    assert q_tile % ATTN_COL_BLOCK == 0
    return pl.pallas_call(
        functools.partial(_diff_attn_kernel, key_tile=key_tile, lambda_init=lambda_init),
        grid=(bsz, C_HEADS, seq // q_tile),
        in_specs=[
            pl.BlockSpec(lam_params.shape, lambda b, h, i: (0, 0)),
            pl.BlockSpec((None, C_VDIM, q_tile), lambda b, h, i: (b, h, i)),
            pl.BlockSpec((None, t_len, C_VDIM), lambda b, h, i: (b, 0, h)),
            pl.BlockSpec((None, C_VDIM, t_len), lambda b, h, i: (b, h, 0)),
            pl.BlockSpec((C_VDIM, 1), lambda b, h, i: (0, 0)),
        ],
        out_specs=pl.BlockSpec((None, q_tile, C_VDIM), lambda b, h, i: (b, i, h)),
        out_shape=jax.ShapeDtypeStruct((bsz, seq, C_WIDTH), F32),
        scratch_shapes=[
            pltpu.VMEM((2, 2, key_tile, q_tile), F32),
            pltpu.VMEM((2, 2, 1, q_tile), F32),
            pltpu.VMEM((2, LANES + ONES_ROWS, q_tile), F32),
            pltpu.VMEM((2, 1, q_tile), F32),
        ],
        compiler_params=_params(("parallel", "parallel", "parallel")),
        name="diff_attention",
    )(lam_params, qt, k, vt, head_norm.reshape(C_VDIM, 1))


def _outproj_odd_kernel(o_ref_in, g_ref, tok_ref, mod_ref, w_ref, o_ref):
    mix = (o_ref_in[...] * _silu(g_ref[...])).astype(BF16)
    o_ref[...] = tok_ref[...] + mod_ref[2:3, :] * _dot(mix, w_ref[...])


def _outproj_odd(o, g, tok, mods, w):
    bsz, seq, _ = o.shape
    ctx_tiles = CTX_LEN // ROW_TILE
    lat = lambda width: pl.BlockSpec((None, ROW_TILE, width), lambda b, i: (b, i, 0))
    cat = lambda width: pl.BlockSpec((None, ROW_TILE, width), lambda b, i: (b, i + ctx_tiles, 0))
    return pl.pallas_call(
        _outproj_odd_kernel,
        grid=(bsz, seq // ROW_TILE),
        in_specs=[
            lat(C_WIDTH), cat(C_WIDTH), cat(D_MODEL),
            pl.BlockSpec((None, None, SUBLANES, D_MODEL), lambda b, i: (b, 1, 0, 0)),
            pl.BlockSpec(w.shape, lambda b, i: (0, 0)),
        ],
        out_specs=lat(D_MODEL),
        out_shape=jax.ShapeDtypeStruct((bsz, seq, D_MODEL), F32),
        compiler_params=_params(("parallel", "parallel")),
        name="outproj_odd",
    )(o, g, tok, mods, w)


def _rope_tables(seq):
    rows = seq // GRID_W
    row = jnp.repeat(jnp.arange(rows), GRID_W).astype(F32)
    col = jnp.tile(jnp.arange(GRID_W), rows).astype(F32)
    n_freq = HEAD_DIM // 4
    inv_freq = ROPE_BASE ** (-jnp.arange(n_freq, dtype=F32) / n_freq)
    ang_r = row[:, None] * inv_freq
    ang_c = col[:, None] * inv_freq
    ang = jnp.concatenate([ang_r, ang_r, ang_c, ang_c], axis=-1)
    sign = jnp.tile(jnp.concatenate([-jnp.ones((n_freq,), F32), jnp.ones((n_freq,), F32)]), 2)
    cos = jnp.concatenate([jnp.ones((CTX_LEN, HEAD_DIM), F32), jnp.cos(ang)], axis=0)
    sin = jnp.concatenate([jnp.zeros((CTX_LEN, HEAD_DIM), F32), jnp.sin(ang) * sign], axis=0)
    reps = LANES // HEAD_DIM
    return jnp.tile(cos, (1, reps)), jnp.tile(sin, (1, reps))


def _pad_lanes(v):
    return jnp.pad(v, ((0, 0), (0, LANES - v.shape[1])))


def kernel(x, c, ctx, c_ctx, mod_w, mod_b, norm_w, ev_w_in, ev_w_out, ev_q_norm, ev_k_norm, ev_sink,
           ev_conv_w, ev_conv_b, ev_dt_bias, ev_a_log, ev_d_skip, ev_ssm_norm, od_w_in, od_w_out,
           od_q_norm, od_k_norm, od_lambda, od_head_norm):
    assert mod_w.shape[0] == DEPTH == 2
    bsz, seq, _ = x.shape
    assert ctx.shape[1] == CTX_LEN and seq % ROW_TILE == 0 and CTX_LEN == ROW_TILE

    cc = jnp.concatenate([c, c_ctx[None], jnp.zeros((SUBLANES - bsz - 1, D_MODEL), F32)], axis=0)
    mod_all = _modulation(cc, mod_w, mod_b)

    def mod_table(li):
        m = mod_all[li].reshape(SUBLANES, 3, D_MODEL)
        rows = jnp.stack([jnp.broadcast_to(m[bsz], (bsz, 3, D_MODEL)), m[:bsz]], axis=1)
        return jnp.pad(rows, ((0, 0), (0, 0), (0, SUBLANES - 3), (0, 0)))

    cos, sin_signed = _rope_tables(seq)
    seg = jnp.arange(LANES) // HEAD_DIM
    seg_ones = (seg[:, None] == seg[None, :]).astype(BF16)
    seg_ones2 = jnp.concatenate([seg_ones, seg_ones], axis=0)
    tile_norm = lambda w: jnp.tile(w, LANES // HEAD_DIM).reshape(1, LANES)

    perm = jnp.array([(hh * A_GROUP + g) * HEAD_DIM + e
                      for g in range(A_GROUP) for hh in range(A_KV_HEADS) for e in range(HEAD_DIM)])
    w_in = ev_w_in[0]
    o_q, o_k, o_v, o_ga, o_z, o_xbc, o_dt = [int(s) for s in
        [0, A_WIDTH, A_WIDTH + A_KV_WIDTH, A_WIDTH + 2 * A_KV_WIDTH, 2 * A_WIDTH + 2 * A_KV_WIDTH,
         2 * A_WIDTH + 2 * A_KV_WIDTH + B_INNER, 2 * A_WIDTH + 2 * A_KV_WIDTH + B_INNER + B_XBC]]
    w0 = jnp.concatenate([
        w_in[:, o_q:o_k][:, perm], w_in[:, o_k:o_ga], w_in[:, o_ga:o_z][:, perm], w_in[:, o_z:],
        jnp.zeros((D_MODEL, LANES - 2 * B_HEADS), F32)], axis=1).astype(BF16)
    q_scale = HEAD_DIM ** -0.5 * LOG2E
    outs0 = (_Out(o_q, A_WIDTH, BF16, "q", q_scale), _Out(o_k, A_KV_WIDTH, BF16, "k"),
             _Out(o_v, A_KV_WIDTH, BF16), _Out(o_ga, A_WIDTH, F32), _Out(o_z, B_INNER, F32),
             _Out(o_xbc, B_XBC, F32), _Out(o_dt, LANES, F32))
    mods0 = mod_table(0)
    q, k, v, ga, z, xbc, dt = _inproj(
        ctx, x, mods0, norm_w[0], w0, cos, sin_signed, tile_norm(ev_q_norm[0]), tile_norm(ev_k_norm[0]),
        seg_ones2, outs0)
    att = _win_attn(ev_sink[0], q, k, v)
    expand = jnp.stack([(jnp.arange(LANES)[:, None] == d * B_HEADS + jnp.arange(B_INNER)[None, :] // B_HEAD_DIM)
                        for d in range(2)]).astype(BF16)
    expand2 = jnp.concatenate([expand, expand], axis=1)
    conv_w = jnp.pad(ev_conv_w[0], ((0, SUBLANES - B_CONV), (0, 0)))
    xs, c_bf, b_t, c_bt, dt_sp = _ssd_prep(xbc, dt, conv_w, ev_conv_b[0].reshape(1, B_XBC),
                                           _pad_lanes(ev_dt_bias[0].reshape(1, 2 * B_HEADS)))
    yf, yb = _ssd(xs, c_bf, b_t, c_bt, dt_sp, _pad_lanes(ev_a_log[0].reshape(1, 2 * B_HEADS)),
                  jnp.repeat(ev_d_skip[0], B_HEAD_DIM).reshape(1, B_INNER), expand2)
    w_out0 = jnp.concatenate([ev_w_out[0][:A_WIDTH][perm], ev_w_out[0][A_WIDTH:]], axis=0).astype(BF16)
    tok = _outproj_even(att, ga, yf, yb, z, ctx, x, mods0, ev_ssm_norm[0], w_out0)

    lambda_init = 0.8 - 0.6 * math.exp(-0.3 * 1)
    outs1 = (_Out(0, C_WIDTH, BF16, "q", q_scale, latent_only=True, transposed=True),
             _Out(C_WIDTH, C_WIDTH, BF16, "k"), _Out(2 * C_WIDTH, C_WIDTH, BF16, transposed=True),
             _Out(3 * C_WIDTH, C_WIDTH, F32))
    mods1 = mod_table(1)
    qt, k, vt, g = _inproj(
        tok, tok, mods1, norm_w[1], od_w_in[0].astype(BF16), cos, sin_signed, tile_norm(od_q_norm[0]),
        tile_norm(od_k_norm[0]), seg_ones2, outs1)
    o = _diff_attn(od_lambda[0], qt, k, vt, od_head_norm[0], lambda_init)
    return _outproj_odd(o, g, tok, mods1, od_w_out[0].astype(BF16))
```

```python
import functools
import math
from typing import NamedTuple

import jax
import jax.numpy as jnp
from jax import lax
from jax.experimental import pallas as pl
from jax.experimental.pallas import tpu as pltpu

D_MODEL = 1024
DEPTH = 2
GRID_W = 64
CTX_LEN = 256
HEAD_DIM = 64
ROPE_BASE = 10000.0
NORM_EPS = 1e-6
BLOCK = 128

A_HEADS = 8
A_KV_HEADS = 2
A_GROUP = A_HEADS // A_KV_HEADS
A_WIDTH = A_HEADS * HEAD_DIM
A_KV_WIDTH = A_KV_HEADS * HEAD_DIM

B_INNER = D_MODEL
B_HEAD_DIM = 64
B_HEADS = B_INNER // B_HEAD_DIM
B_GROUPS = 2
B_HPG = B_HEADS // B_GROUPS
B_STATE = 128
B_CONV = 5
B_CHUNK = 128
B_XBC = B_INNER + 2 * B_GROUPS * B_STATE

C_HEADS = D_MODEL // (2 * HEAD_DIM)
C_VDIM = 2 * HEAD_DIM
C_WIDTH = C_HEADS * C_VDIM

LANES = 128
SUBLANES = 8
ROW_TILE = 256
ATTN_Q_TILE = 2048
ATTN_KEY_TILE = 768
ATTN_COL_BLOCK = 256
VMEM_LIMIT = 56 * 1024 * 1024

F32 = jnp.float32
BF16 = jnp.bfloat16
NEG_INF = float("-inf")
LOG2E = math.log2(math.e)


def _silu(x):
    return x * (1.0 / (1.0 + jnp.exp(-x)))


def _dot(a, b):
    return jnp.dot(a, b, preferred_element_type=F32)


def _dot_nt(a, b):
    return lax.dot_general(a, b, (((1,), (1,)), ((), ())), preferred_element_type=F32)


def _split2(x):
    hi = x.astype(BF16)
    lo = (x - hi.astype(F32)).astype(BF16)
    return hi, lo


def _split3(x):
    hi = x.astype(BF16)
    r = x - hi.astype(F32)
    mid = r.astype(BF16)
    lo = (r - mid.astype(F32)).astype(BF16)
    return hi, mid, lo


def _params(sem):
    return pltpu.CompilerParams(dimension_semantics=sem, vmem_limit_bytes=VMEM_LIMIT)


def _mod_kernel(cc_ref, w_ref, b_ref, o_ref):
    s = _silu(cc_ref[...])
    o_ref[...] = jnp.dot(s, w_ref[...], precision=lax.Precision.HIGHEST,
                         preferred_element_type=F32) + b_ref[...]


def _modulation(cc, mod_w, mod_b):
    depth = mod_w.shape[0]
    n_tiles = 3
    return pl.pallas_call(
        _mod_kernel,
        grid=(depth, n_tiles),
        in_specs=[
            pl.BlockSpec((SUBLANES, D_MODEL), lambda l, n: (0, 0)),
            pl.BlockSpec((None, D_MODEL, D_MODEL), lambda l, n: (l, 0, n)),
            pl.BlockSpec((None, 1, D_MODEL), lambda l, n: (l, 0, n)),
        ],
        out_specs=pl.BlockSpec((None, SUBLANES, D_MODEL), lambda l, n: (l, 0, n)),
        out_shape=jax.ShapeDtypeStruct((depth, SUBLANES, 3 * D_MODEL), F32),
        compiler_params=_params(("arbitrary", "arbitrary")),
        name="modulation",
    )(cc, mod_w, mod_b.reshape(depth, 1, 3 * D_MODEL))


class _Out(NamedTuple):
    start: int
    width: int
    dtype: object
    kind: str = "plain"
    scale: float = 1.0
    latent_only: bool = False
    transposed: bool = False


def _head_norm_rope(a, nw, cos, sin_signed, seg_ones2, first_quarter):
    hi, lo = _split2(a * a)
    ssq = _dot(jnp.concatenate([hi, lo], axis=1), seg_ones2)
    n = a * lax.rsqrt(ssq * (1.0 / HEAD_DIM) + NORM_EPS) * nw
    quarter = HEAD_DIM // 4
    rot = jnp.where(first_quarter, pltpu.roll(n, LANES - quarter, 1), pltpu.roll(n, quarter, 1))
    return n * cos + rot * sin_signed


def _inproj_kernel(ctx_ref, lat_ref, mod_ref, nw_ref, w_ref, cos_ref, sin_ref, qn_ref, kn_ref, ones_ref,
                   *refs, outs, conv):
    if conv is not None:
        xprev_ref, xnext_ref, cw_ref, cbias_ref, dtb_ref = refs[:5]
        refs = refs[5:]
    out_refs = refs[:len(outs)]

    def modulated(x):
        ms = jnp.mean(x * x, axis=-1, keepdims=True)
        y = x * lax.rsqrt(ms + NORM_EPS) * nw_ref[...]
        return (y * (1.0 + mod_ref[1:2, :]) + mod_ref[0:1, :]).astype(BF16)

    x = jnp.where(pl.program_id(1) < CTX_LEN // ROW_TILE, ctx_ref[...], lat_ref[...])
    hb = modulated(x)
    lane = lax.broadcasted_iota(jnp.int32, (x.shape[0], LANES), 1)
    first_quarter = (lane % (HEAD_DIM // 2)) < (HEAD_DIM // 4)
    for out, o_ref in zip(outs, out_refs):
        wide = _dot(hb, w_ref[:, out.start:out.start + out.width])
        if out.kind == "plain" and not out.transposed:
            o_ref[...] = wide.astype(out.dtype)
            continue
        for s in range(out.width // LANES):
            cols = slice(s * LANES, (s + 1) * LANES)
            r = wide[:, cols]
            if out.kind != "plain":
                nw = qn_ref[...] if out.kind == "q" else kn_ref[...]
                r = _head_norm_rope(r, nw, cos_ref[...], sin_ref[...], ones_ref[...], first_quarter) * out.scale
            if out.transposed:
                o_ref[cols, :] = r.astype(out.dtype).T
            else:
                o_ref[:, cols] = r.astype(out.dtype)
    if conv is not None:
        hb_halo = modulated(jnp.concatenate([xprev_ref[...], xnext_ref[...]], axis=0))
        _ssd_inputs(hb, hb_halo, w_ref, cw_ref, cbias_ref, dtb_ref, conv, *refs[len(outs):])


def _ssd_inputs(hb, hb_halo, w_ref, cw_ref, cbias_ref, dtb_ref, conv, xs_ref, c_ref, bt_ref, cbm_ref, dtsp_ref):
    xbc_start, dt_start, n_row = conv
    i = pl.program_id(1)
    ctx_tiles = CTX_LEN // ROW_TILE
    w_xbc = w_ref[:, xbc_start:xbc_start + B_XBC]
    xbc = _dot(hb, w_xbc)
    halo = _dot(hb_halo, w_xbc)
    keep_prev = jnp.where(i <= ctx_tiles, 0.0, 1.0)
    keep_next = jnp.where((i == ctx_tiles - 1) | (i == n_row - 1), 0.0, 1.0)
    ext = jnp.concatenate([halo[:SUBLANES] * keep_prev, xbc, halo[SUBLANES:] * keep_next], axis=0)
    n_ext = ROW_TILE + 2 * SUBLANES
    acc = cbias_ref[...]
    for j in range(B_CONV):
        shifted = ext if j == B_CONV // 2 else pltpu.roll(ext, (B_CONV // 2 - j) % n_ext, 0)
        acc = acc + shifted[SUBLANES:SUBLANES + ROW_TILE, :] * cw_ref[j:j + 1, :]
    act = _silu(acc)
    xs_ref[...] = act[:, :B_INNER]
    for g in range(B_GROUPS):
        b_g = act[:, B_INNER + g * B_STATE:B_INNER + (g + 1) * B_STATE]
        c_g = act[:, B_INNER + (B_GROUPS + g) * B_STATE:B_INNER + (B_GROUPS + g + 1) * B_STATE].astype(BF16)
        c_ref[:, g * B_STATE:(g + 1) * B_STATE] = c_g
        for ch in range(ROW_TILE // B_CHUNK):
            rows = slice(ch * B_CHUNK, (ch + 1) * B_CHUNK)
            bt_ref[ch, g * B_STATE:(g + 1) * B_STATE, :] = b_g[rows].T.astype(BF16)
            cbm_ref[ch, g] = _dot_nt(c_g[rows], b_g[rows].astype(BF16))
    raw = _dot(hb, w_ref[:, dt_start:dt_start + LANES]) + dtb_ref[...]
    dtsp_ref[...] = jnp.maximum(raw, 0.0) + jnp.log(1.0 + jnp.exp(-jnp.abs(raw)))


def _inproj(ctx_src, lat_src, mods, norm_w, w, cos, sin_signed, q_norm, k_norm, seg_ones2, outs, conv=None):
    bsz = lat_src.shape[0]
    ctx_tiles = CTX_LEN // ROW_TILE
    split = ctx_src is not lat_src
    t_len = CTX_LEN + lat_src.shape[1] if split else lat_src.shape[1]
    n_row = t_len // ROW_TILE
    lat_row = (lambda i: jnp.maximum(i - ctx_tiles, 0)) if split else (lambda i: i)
    const = lambda shape: pl.BlockSpec(shape, lambda b, i: (0,) * len(shape))

    def out_spec(out):
        r = (lambda i: jnp.maximum(i - ctx_tiles, 0)) if out.latent_only else (lambda i: i)
        if out.transposed:
            return pl.BlockSpec((None, out.width, ROW_TILE), lambda b, i: (b, 0, r(i)))
        return pl.BlockSpec((None, ROW_TILE, out.width), lambda b, i: (b, r(i), 0))

    def out_shape(out):
        rows = t_len - CTX_LEN if out.latent_only else t_len
        return jax.ShapeDtypeStruct((bsz, out.width, rows) if out.transposed else (bsz, rows, out.width), out.dtype)

    in_specs = [
        pl.BlockSpec((None, ROW_TILE, D_MODEL), lambda b, i: (b, 0, 0)),
        pl.BlockSpec((None, ROW_TILE, D_MODEL), lambda b, i: (b, lat_row(i), 0)),
        pl.BlockSpec((None, None, SUBLANES, D_MODEL), lambda b, i: (b, jnp.minimum(i, 1), 0, 0)),
        const((1, D_MODEL)),
        const(w.shape),
        pl.BlockSpec((ROW_TILE, LANES), lambda b, i: (i, 0)),
        pl.BlockSpec((ROW_TILE, LANES), lambda b, i: (i, 0)),
        const((1, LANES)),
        const((1, LANES)),
        const((2 * LANES, LANES)),
    ]
    args = [ctx_src, lat_src, mods, norm_w.reshape(1, D_MODEL), w, cos, sin_signed, q_norm, k_norm, seg_ones2]
    out_specs = [out_spec(o) for o in outs]
    out_shapes = [out_shape(o) for o in outs]
    conv_cfg = None
    if conv is not None:
        assert split
        xbc_start, dt_start, conv_w, conv_b, dt_bias = conv
        conv_cfg = (xbc_start, dt_start, n_row)
        rows8 = ROW_TILE // SUBLANES
        n_rows8 = lat_src.shape[1] // SUBLANES
        chunks = ROW_TILE // B_CHUNK
        n_chunks = t_len // B_CHUNK
        halo = lambda fn: pl.BlockSpec((None, SUBLANES, D_MODEL), fn)
        in_specs += [
            halo(lambda b, i: (b, jnp.maximum((i - ctx_tiles) * rows8 - 1, 0), 0)),
            halo(lambda b, i: (b, jnp.clip((i - ctx_tiles + 1) * rows8, 0, n_rows8 - 1), 0)),
            const((SUBLANES, B_XBC)), const((1, B_XBC)), const((1, LANES)),
        ]
        args += [lat_src, lat_src, conv_w, conv_b, dt_bias]
        row = lambda width: pl.BlockSpec((None, ROW_TILE, width), lambda b, i: (b, i, 0))
        out_specs += [
            row(B_INNER), row(B_GROUPS * B_STATE),
            pl.BlockSpec((None, chunks, B_GROUPS * B_STATE, B_CHUNK), lambda b, i: (b, i, 0, 0)),
            pl.BlockSpec((None, chunks, B_GROUPS, B_CHUNK, B_CHUNK), lambda b, i: (b, i, 0, 0, 0)),
            row(LANES),
        ]
        out_shapes += [
            jax.ShapeDtypeStruct((bsz, t_len, B_INNER), F32),
            jax.ShapeDtypeStruct((bsz, t_len, B_GROUPS * B_STATE), BF16),
            jax.ShapeDtypeStruct((bsz, n_chunks, B_GROUPS * B_STATE, B_CHUNK), BF16),
            jax.ShapeDtypeStruct((bsz, n_chunks, B_GROUPS, B_CHUNK, B_CHUNK), F32),
            jax.ShapeDtypeStruct((bsz, t_len, LANES), F32),
        ]
    return pl.pallas_call(
        functools.partial(_inproj_kernel, outs=outs, conv=conv_cfg),
        grid=(bsz, n_row),
        in_specs=in_specs,
        out_specs=out_specs,
        out_shape=out_shapes,
        compiler_params=_params(("parallel", "arbitrary")),
        name="inproj",
    )(*args)


def _win_attn_kernel(sink_ref, q_ref, kp_ref, kc_ref, kn_ref, kx_ref, vp_ref, vc_ref, vn_ref, vx_ref,
                     bias_ref, o_ref):
    kk = jnp.concatenate([kp_ref[...], kc_ref[...], kn_ref[...], kx_ref[...]], axis=0)
    vv = jnp.concatenate([vp_ref[...], vc_ref[...], vn_ref[...], vx_ref[...]], axis=0)
    n_keys = kk.shape[0]
    v1 = jnp.concatenate([vv, jnp.ones((n_keys, LANES), BF16)], axis=1)
    rows = A_GROUP * BLOCK
    bias = bias_ref[...]
    lane = lax.broadcasted_iota(jnp.int32, (BLOCK, LANES), 1)
    head_of_lane = lane // HEAD_DIM
    g_of_row = lax.broadcasted_iota(jnp.int32, (rows, 1), 0) // BLOCK
    outs = []
    for hh in range(A_KV_HEADS):
        qs = jnp.concatenate(
            [jnp.where(head_of_lane == hh, q_ref[:, g * LANES:(g + 1) * LANES], jnp.zeros((BLOCK, LANES), BF16))
             for g in range(A_GROUP)], axis=0)
        s = _dot_nt(qs, kk).reshape(A_GROUP, BLOCK, n_keys) + bias[None]
        s = s.reshape(rows, n_keys)
        sink = jnp.zeros((rows, 1), F32)
        for g in range(A_GROUP):
            sink = jnp.where(g_of_row == g, sink_ref[hh * A_GROUP + g] * LOG2E, sink)
        m = jnp.maximum(jnp.max(s, axis=-1, keepdims=True), sink)
        e = jnp.exp2(s - m).astype(BF16)
        ov = _dot(e, v1)
        den = jnp.exp2(sink - m) + ov[:, LANES:]
        outs.append(ov[:, :LANES] * (1.0 / den))
    for g in range(A_GROUP):
        o_ref[:, g * LANES:(g + 1) * LANES] = jnp.where(
            head_of_lane == 0, outs[0][g * BLOCK:(g + 1) * BLOCK], outs[1][g * BLOCK:(g + 1) * BLOCK])


def _win_bias():
    t = jnp.arange(BLOCK)[:, None]
    c = jnp.arange(3 * BLOCK + CTX_LEN)[None, :]
    piece = c // BLOCK
    offset = c - t - BLOCK
    window = (offset >= -BLOCK) & (offset <= BLOCK)
    ctx_keys = jnp.broadcast_to(piece >= 3, window.shape)
    kinds = [
        ctx_keys,
        ctx_keys | (window & (piece >= 1) & (piece < 3)),
        ctx_keys | (window & (piece < 3)),
        ctx_keys | (window & (piece < 2)),
    ]
    return jnp.where(jnp.stack(kinds), 0.0, NEG_INF).astype(F32)


def _win_attn(sink, q, k, v):
    bsz, t_len, _ = q.shape
    n_blocks = t_len // BLOCK
    n_ctx_blocks = CTX_LEN // BLOCK
    assert n_blocks - n_ctx_blocks >= 2
    kv = lambda fn: pl.BlockSpec((None, BLOCK, A_KV_WIDTH), fn)
    prev = lambda b, i: (b, jnp.maximum(i - 1, 0), 0)
    cur = lambda b, i: (b, i, 0)
    nxt = lambda b, i: (b, jnp.minimum(i + 1, n_blocks - 1), 0)
    ctx = pl.BlockSpec((None, CTX_LEN, A_KV_WIDTH), lambda b, i: (b, 0, 0))

    def kind(b, i):
        latent = jnp.where(i == n_ctx_blocks, 1, jnp.where(i == n_blocks - 1, 3, 2))
        return (jnp.where(i < n_ctx_blocks, 0, latent), 0, 0)

    return pl.pallas_call(
        _win_attn_kernel,
        grid=(bsz, n_blocks),
        in_specs=[
            pl.BlockSpec(memory_space=pltpu.SMEM),
            pl.BlockSpec((None, BLOCK, A_WIDTH), cur),
            kv(prev), kv(cur), kv(nxt), ctx,
            kv(prev), kv(cur), kv(nxt), ctx,
            pl.BlockSpec((None, BLOCK, 3 * BLOCK + CTX_LEN), kind),
        ],
        out_specs=pl.BlockSpec((None, BLOCK, A_WIDTH), cur),
        out_shape=jax.ShapeDtypeStruct((bsz, t_len, A_WIDTH), F32),
        compiler_params=_params(("parallel", "parallel")),
        name="window_attention",
    )(sink, q, k, k, k, k, v, v, v, v, _win_bias())


def _ssd_direction(d, xs_ref, c_ref, bt_ref, cbm_ref, dt_ref, alog_ref, expand_ref, state_ref):
    backward = d == 1
    xs = xs_ref[...]
    lane = lax.broadcasted_iota(jnp.int32, (B_CHUNK, LANES), 1)
    row = lax.broadcasted_iota(jnp.int32, (B_CHUNK, LANES), 0)
    dir_lanes = (lane >= d * B_HEADS) & (lane < (d + 1) * B_HEADS)
    dt = dt_ref[...]
    a = -jnp.exp(alog_ref[...])
    dta = jnp.where(dir_lanes, dt * a, 0.0)
    tri_mask = (lane >= row) if backward else (lane <= row)
    tri = jnp.where(tri_mask, 1.0, 0.0).astype(BF16)
    tri_t = jnp.where((row >= lane) if backward else (row <= lane), 1.0, 0.0).astype(BF16)
    la_col = sum(_dot(tri, p) for p in _split3(dta))
    la_row = sum(_dot(p, tri_t) for p in _split3(dta.T))
    last = 0 if backward else B_CHUNK - 1
    ela = jnp.exp(la_col)
    w_end = jnp.exp(la_col[last:last + 1, :] - la_col)
    expand2 = expand_ref[d]

    def widen(v):
        hi, lo = _split2(jnp.where(dir_lanes, v, 0.0))
        return _dot(jnp.concatenate([hi, lo], axis=1), expand2)

    dt_e = widen(dt)
    ela_e = widen(ela)
    wdt_e = widen(w_end * dt)
    xdt = (xs * dt_e).astype(BF16)
    wx = (xs * wdt_e).astype(BF16)
    half_of_lane = lane // B_HEAD_DIM
    slabs = []
    gw = B_HPG * B_HEAD_DIM
    for g in range(B_GROUPS):
        c_g = c_ref[:, g * B_STATE:(g + 1) * B_STATE]
        cb = cbm_ref[g]
        h_in = state_ref[d, g]
        y_inter = _dot(c_g, h_in.astype(BF16)) * ela_e[:, g * gw:(g + 1) * gw]
        for pair in range(B_HPG // 2):
            col0 = g * gw + pair * LANES
            xdt_pair = xdt[:, col0:col0 + LANES]
            acc = y_inter[:, pair * LANES:(pair + 1) * LANES]
            for e in range(2):
                col = d * B_HEADS + g * B_HPG + 2 * pair + e
                seg = la_col[:, col:col + 1] - la_row[col:col + 1, :]
                decay = jnp.exp(jnp.where(tri_mask, seg, NEG_INF))
                m = (cb * decay).astype(BF16)
                rhs = jnp.where(half_of_lane == e, xdt_pair, jnp.zeros_like(xdt_pair))
                acc = acc + _dot(m, rhs)
            slabs.append(acc)
        chunk_decay = ela_e[last:last + 1, g * gw:(g + 1) * gw]
        state_ref[d, g] = h_in * chunk_decay + _dot(bt_ref[g * B_STATE:(g + 1) * B_STATE, :], wx[:, g * gw:(g + 1) * gw])
    return slabs, xs


def _ssd_kernel(xsf_ref, cf_ref, btf_ref, cbf_ref, dtf_ref, xsb_ref, cb_ref, btb_ref, cbb_ref, dtb_ref,
                alog_ref, dskip_ref, expand_ref, yf_ref, yb_ref, state_ref):
    @pl.when(pl.program_id(1) == 0)
    def _():
        state_ref[...] = jnp.zeros_like(state_ref)

    common = (alog_ref, expand_ref, state_ref)
    slabs, xs = _ssd_direction(0, xsf_ref, cf_ref, btf_ref, cbf_ref, dtf_ref, *common)
    for n, slab in enumerate(slabs):
        sl = slice(n * LANES, (n + 1) * LANES)
        yf_ref[:, sl] = slab + dskip_ref[:, sl] * xs[:, sl]
    slabs, _ = _ssd_direction(1, xsb_ref, cb_ref, btb_ref, cbb_ref, dtb_ref, *common)
    for n, slab in enumerate(slabs):
        yb_ref[:, n * LANES:(n + 1) * LANES] = slab


def _ssd(xs, c, bt, cbm, dt, a_log, d_skip_e, expand2):
    bsz, t_len, _ = xs.shape
    n_chunks = t_len // B_CHUNK
    n_ctx_chunks = CTX_LEN // B_CHUNK

    def chunk_b(j):
        return jnp.where(j < n_ctx_chunks, n_ctx_chunks - 1 - j, n_chunks + n_ctx_chunks - 1 - j)

    def specs(chunk_of):
        return [
            pl.BlockSpec((None, B_CHUNK, B_INNER), lambda b, j: (b, chunk_of(j), 0)),
            pl.BlockSpec((None, B_CHUNK, B_GROUPS * B_STATE), lambda b, j: (b, chunk_of(j), 0)),
            pl.BlockSpec((None, None, B_GROUPS * B_STATE, B_CHUNK), lambda b, j: (b, chunk_of(j), 0, 0)),
            pl.BlockSpec((None, None, B_GROUPS, B_CHUNK, B_CHUNK), lambda b, j: (b, chunk_of(j), 0, 0, 0)),
            pl.BlockSpec((None, B_CHUNK, LANES), lambda b, j: (b, chunk_of(j), 0)),
        ]

    const = lambda shape: pl.BlockSpec(shape, lambda b, j: (0,) * len(shape))
    y_spec = lambda chunk_of: pl.BlockSpec((None, B_CHUNK, B_INNER), lambda b, j: (b, chunk_of(j), 0))
    y_shape = jax.ShapeDtypeStruct((bsz, t_len, B_INNER), F32)
    return pl.pallas_call(
        _ssd_kernel,
        grid=(bsz, n_chunks),
        in_specs=specs(lambda j: j) + specs(chunk_b) + [
            const((1, LANES)), const((1, B_INNER)), const((2, 2 * LANES, B_INNER)),
        ],
        out_specs=[y_spec(lambda j: j), y_spec(chunk_b)],
        out_shape=[y_shape, y_shape],
        scratch_shapes=[pltpu.VMEM((2, B_GROUPS, B_STATE, B_HPG * B_HEAD_DIM), F32)],
        compiler_params=_params(("parallel", "arbitrary")),
        name="ssd_scan",
    )(xs, c, bt, cbm, dt, xs, c, bt, cbm, dt, a_log, d_skip_e, expand2)


def _outproj_even_kernel(att_ref, ga_ref, yf_ref, yb_ref, z_ref, ctx_ref, lat_ref, mod_ref, sn_ref, w_ref, o_ref):
    tok = jnp.where(pl.program_id(1) < CTX_LEN // ROW_TILE, ctx_ref[...], lat_ref[...])
    att = (att_ref[...] * _silu(ga_ref[...])).astype(BF16)
    y = (yf_ref[...] + yb_ref[...]) * _silu(z_ref[...])
    ms = jnp.mean(y * y, axis=-1, keepdims=True)
    yn = (y * lax.rsqrt(ms + NORM_EPS) * sn_ref[...]).astype(BF16)
    out = _dot(att, w_ref[:A_WIDTH, :]) + _dot(yn, w_ref[A_WIDTH:, :])
    o_ref[...] = tok + mod_ref[2:3, :] * out


def _outproj_even(att, ga, yf, yb, z, ctx, x, mods, ssm_norm, w):
    bsz, t_len, _ = att.shape
    ctx_tiles = CTX_LEN // ROW_TILE
    row = lambda width: pl.BlockSpec((None, ROW_TILE, width), lambda b, i: (b, i, 0))
    const = lambda shape: pl.BlockSpec(shape, lambda b, i: (0,) * len(shape))
    return pl.pallas_call(
        _outproj_even_kernel,
        grid=(bsz, t_len // ROW_TILE),
        in_specs=[
            row(A_WIDTH), row(A_WIDTH), row(B_INNER), row(B_INNER), row(B_INNER),
            pl.BlockSpec((None, ROW_TILE, D_MODEL), lambda b, i: (b, 0, 0)),
            pl.BlockSpec((None, ROW_TILE, D_MODEL), lambda b, i: (b, jnp.maximum(i - ctx_tiles, 0), 0)),
            pl.BlockSpec((None, None, SUBLANES, D_MODEL), lambda b, i: (b, jnp.minimum(i, 1), 0, 0)),
            const((1, B_INNER)), const(w.shape),
        ],
        out_specs=row(D_MODEL),
        out_shape=jax.ShapeDtypeStruct((bsz, t_len, D_MODEL), F32),
        compiler_params=_params(("parallel", "parallel")),
        name="outproj_even",
    )(att, ga, yf, yb, z, ctx, x, mods, ssm_norm.reshape(1, B_INNER), w)


ONES_ROWS = 16


def _diff_attn_kernel(lam_ref, qt_ref, k_ref, vt_ref, hn_ref, o_ref, s_ref, mx_ref, acc_ref, m_ref,
                      *, key_tile, lambda_init):
    tq = qt_ref.shape[1]
    n_tiles = k_ref.shape[0] // key_tile
    n_cols = tq // ATTN_COL_BLOCK
    row = lax.broadcasted_iota(jnp.int32, (LANES, tq), 0)
    qt = qt_ref[...]
    zero = jnp.zeros_like(qt)
    qz = [jnp.where((row // HEAD_DIM) == c, qt, zero) for c in range(2)]
    ones = jnp.ones((ONES_ROWS, key_tile), BF16)

    def scores(t, slot, n):
        cols = slice(n * ATTN_COL_BLOCK, (n + 1) * ATTN_COL_BLOCK)
        start = pl.multiple_of(t * key_tile, key_tile)
        k = k_ref[pl.ds(start, key_tile), :]
        for c in range(2):
            s = _dot(k, qz[c][:, cols])
            s_ref[slot, c, :, cols] = s
            mx_ref[slot, c, :, cols] = jnp.max(s, axis=0, keepdims=True)

    def consume(t, slot, n):
        cols = slice(n * ATTN_COL_BLOCK, (n + 1) * ATTN_COL_BLOCK)
        start = pl.multiple_of(t * key_tile, key_tile)
        v1 = jnp.concatenate([vt_ref[:, pl.ds(start, key_tile)], ones], axis=0)
        for c in range(2):
            s = s_ref[slot, c, :, cols]
            m_old = m_ref[c, :, cols]
            m_new = jnp.maximum(m_old, mx_ref[slot, c, :, cols])
            alpha = jnp.exp2(m_old - m_new)
            p = jnp.exp2(s - m_new).astype(BF16)
            acc_ref[c, :, cols] = alpha * acc_ref[c, :, cols] + _dot(v1, p)
            m_ref[c, :, cols] = m_new

    m_ref[...] = jnp.full(m_ref.shape, NEG_INF, F32)
    acc_ref[...] = jnp.zeros(acc_ref.shape, F32)
    for n in range(n_cols):
        scores(0, 0, n)

    def body(tt, carry):
        for n in range(n_cols):
            scores(2 * tt + 1, 1, n)
            consume(2 * tt, 0, n)
        for n in range(n_cols):
            scores(2 * tt + 2, 0, n)
            consume(2 * tt + 1, 1, n)
        return carry

    lax.fori_loop(0, (n_tiles - 1) // 2, body, 0)
    for n in range(n_cols):
        consume(n_tiles - 1, 0, n)

    lp = lam_ref[...]
    lam = (jnp.exp(jnp.sum(lp[0:1] * lp[1:2], axis=-1, keepdims=True))
           - jnp.exp(jnp.sum(lp[2:3] * lp[3:4], axis=-1, keepdims=True)) + lambda_init)
    a0 = acc_ref[0]
    a1 = acc_ref[1]
    o = a0[:LANES] * (1.0 / a0[LANES:LANES + 1]) - lam * (a1[:LANES] * (1.0 / a1[LANES:LANES + 1]))
    ms = jnp.mean(o * o, axis=0, keepdims=True)
    o = o * lax.rsqrt(ms + NORM_EPS) * hn_ref[...] * (1.0 - lambda_init)
    o_ref[...] = o.T


def _diff_attn(lam_params, qt, k, vt, head_norm, lambda_init):
    bsz, _, seq = qt.shape
    t_len = k.shape[1]
    q_tile = ATTN_Q_TILE
    key_tile = ATTN_KEY_TILE
    assert seq % q_tile == 0 and t_len % key_tile == 0 and (t_len // key_tile) % 2 == 1
    assert q_tile % ATTN_COL_BLOCK == 0
    return pl.pallas_call(
        functools.partial(_diff_attn_kernel, key_tile=key_tile, lambda_init=lambda_init),
        grid=(bsz, C_HEADS, seq // q_tile),
        in_specs=[
            pl.BlockSpec(lam_params.shape, lambda b, h, i: (0, 0)),
            pl.BlockSpec((None, C_VDIM, q_tile), lambda b, h, i: (b, h, i)),
            pl.BlockSpec((None, t_len, C_VDIM), lambda b, h, i: (b, 0, h)),
            pl.BlockSpec((None, C_VDIM, t_len), lambda b, h, i: (b, h, 0)),
            pl.BlockSpec((C_VDIM, 1), lambda b, h, i: (0, 0)),
        ],
        out_specs=pl.BlockSpec((None, q_tile, C_VDIM), lambda b, h, i: (b, i, h)),
        out_shape=jax.ShapeDtypeStruct((bsz, seq, C_WIDTH), F32),
        scratch_shapes=[
            pltpu.VMEM((2, 2, key_tile, q_tile), F32),
            pltpu.VMEM((2, 2, 1, q_tile), F32),
            pltpu.VMEM((2, LANES + ONES_ROWS, q_tile), F32),
            pltpu.VMEM((2, 1, q_tile), F32),
        ],
        compiler_params=_params(("parallel", "parallel", "parallel")),
        name="diff_attention",
    )(lam_params, qt, k, vt, head_norm.reshape(C_VDIM, 1))


def _outproj_odd_kernel(o_ref_in, g_ref, tok_ref, mod_ref, w_ref, o_ref):
    mix = (o_ref_in[...] * _silu(g_ref[...])).astype(BF16)
    o_ref[...] = tok_ref[...] + mod_ref[2:3, :] * _dot(mix, w_ref[...])


def _outproj_odd(o, g, tok, mods, w):
    bsz, seq, _ = o.shape
    ctx_tiles = CTX_LEN // ROW_TILE
    lat = lambda width: pl.BlockSpec((None, ROW_TILE, width), lambda b, i: (b, i, 0))
    cat = lambda width: pl.BlockSpec((None, ROW_TILE, width), lambda b, i: (b, i + ctx_tiles, 0))
    return pl.pallas_call(
        _outproj_odd_kernel,
        grid=(bsz, seq // ROW_TILE),
        in_specs=[
            lat(C_WIDTH), cat(C_WIDTH), cat(D_MODEL),
            pl.BlockSpec((None, None, SUBLANES, D_MODEL), lambda b, i: (b, 1, 0, 0)),
            pl.BlockSpec(w.shape, lambda b, i: (0, 0)),
        ],
        out_specs=lat(D_MODEL),
        out_shape=jax.ShapeDtypeStruct((bsz, seq, D_MODEL), F32),
        compiler_params=_params(("parallel", "parallel")),
        name="outproj_odd",
    )(o, g, tok, mods, w)


def _rope_tables(seq):
    rows = seq // GRID_W
    row = jnp.repeat(jnp.arange(rows), GRID_W).astype(F32)
    col = jnp.tile(jnp.arange(GRID_W), rows).astype(F32)
    n_freq = HEAD_DIM // 4
    inv_freq = ROPE_BASE ** (-jnp.arange(n_freq, dtype=F32) / n_freq)
    ang_r = row[:, None] * inv_freq
    ang_c = col[:, None] * inv_freq
    ang = jnp.concatenate([ang_r, ang_r, ang_c, ang_c], axis=-1)
    sign = jnp.tile(jnp.concatenate([-jnp.ones((n_freq,), F32), jnp.ones((n_freq,), F32)]), 2)
    cos = jnp.concatenate([jnp.ones((CTX_LEN, HEAD_DIM), F32), jnp.cos(ang)], axis=0)
    sin = jnp.concatenate([jnp.zeros((CTX_LEN, HEAD_DIM), F32), jnp.sin(ang) * sign], axis=0)
    reps = LANES // HEAD_DIM
    return jnp.tile(cos, (1, reps)), jnp.tile(sin, (1, reps))


def _pad_lanes(v):
    return jnp.pad(v, ((0, 0), (0, LANES - v.shape[1])))


def kernel(x, c, ctx, c_ctx, mod_w, mod_b, norm_w, ev_w_in, ev_w_out, ev_q_norm, ev_k_norm, ev_sink,
           ev_conv_w, ev_conv_b, ev_dt_bias, ev_a_log, ev_d_skip, ev_ssm_norm, od_w_in, od_w_out,
           od_q_norm, od_k_norm, od_lambda, od_head_norm):
    assert mod_w.shape[0] == DEPTH == 2
    bsz, seq, _ = x.shape
    assert ctx.shape[1] == CTX_LEN and seq % ROW_TILE == 0 and CTX_LEN == ROW_TILE

    cc = jnp.concatenate([c, c_ctx[None], jnp.zeros((SUBLANES - bsz - 1, D_MODEL), F32)], axis=0)
    mod_all = _modulation(cc, mod_w, mod_b)

    def mod_table(li):
        m = mod_all[li].reshape(SUBLANES, 3, D_MODEL)
        rows = jnp.stack([jnp.broadcast_to(m[bsz], (bsz, 3, D_MODEL)), m[:bsz]], axis=1)
        return jnp.pad(rows, ((0, 0), (0, 0), (0, SUBLANES - 3), (0, 0)))

    cos, sin_signed = _rope_tables(seq)
    seg = jnp.arange(LANES) // HEAD_DIM
    seg_ones = (seg[:, None] == seg[None, :]).astype(BF16)
    seg_ones2 = jnp.concatenate([seg_ones, seg_ones], axis=0)
    tile_norm = lambda w: jnp.tile(w, LANES // HEAD_DIM).reshape(1, LANES)

    perm = jnp.array([(hh * A_GROUP + g) * HEAD_DIM + e
                      for g in range(A_GROUP) for hh in range(A_KV_HEADS) for e in range(HEAD_DIM)])
    w_in = ev_w_in[0]
    o_q, o_k, o_v, o_ga, o_z, o_xbc, o_dt = [int(s) for s in
        [0, A_WIDTH, A_WIDTH + A_KV_WIDTH, A_WIDTH + 2 * A_KV_WIDTH, 2 * A_WIDTH + 2 * A_KV_WIDTH,
         2 * A_WIDTH + 2 * A_KV_WIDTH + B_INNER, 2 * A_WIDTH + 2 * A_KV_WIDTH + B_INNER + B_XBC]]
    w0 = jnp.concatenate([
        w_in[:, o_q:o_k][:, perm], w_in[:, o_k:o_ga], w_in[:, o_ga:o_z][:, perm], w_in[:, o_z:],
        jnp.zeros((D_MODEL, LANES - 2 * B_HEADS), F32)], axis=1).astype(BF16)
    q_scale = HEAD_DIM ** -0.5 * LOG2E
    outs0 = (_Out(o_q, A_WIDTH, BF16, "q", q_scale), _Out(o_k, A_KV_WIDTH, BF16, "k"),
             _Out(o_v, A_KV_WIDTH, BF16), _Out(o_ga, A_WIDTH, F32), _Out(o_z, B_INNER, F32))
    mods0 = mod_table(0)
    conv_w = jnp.pad(ev_conv_w[0], ((0, SUBLANES - B_CONV), (0, 0)))
    q, k, v, ga, z, xs, c_bf, b_t, c_bt, dt_sp = _inproj(
        ctx, x, mods0, norm_w[0], w0, cos, sin_signed, tile_norm(ev_q_norm[0]), tile_norm(ev_k_norm[0]),
        seg_ones2, outs0,
        conv=(o_xbc, o_dt, conv_w, ev_conv_b[0].reshape(1, B_XBC), _pad_lanes(ev_dt_bias[0].reshape(1, 2 * B_HEADS))))
    att = _win_attn(ev_sink[0], q, k, v)
    expand = jnp.stack([(jnp.arange(LANES)[:, None] == d * B_HEADS + jnp.arange(B_INNER)[None, :] // B_HEAD_DIM)
                        for d in range(2)]).astype(BF16)
    expand2 = jnp.concatenate([expand, expand], axis=1)
    yf, yb = _ssd(xs, c_bf, b_t, c_bt, dt_sp, _pad_lanes(ev_a_log[0].reshape(1, 2 * B_HEADS)),
                  jnp.repeat(ev_d_skip[0], B_HEAD_DIM).reshape(1, B_INNER), expand2)
    w_out0 = jnp.concatenate([ev_w_out[0][:A_WIDTH][perm], ev_w_out[0][A_WIDTH:]], axis=0).astype(BF16)
    tok = _outproj_even(att, ga, yf, yb, z, ctx, x, mods0, ev_ssm_norm[0], w_out0)

    lambda_init = 0.8 - 0.6 * math.exp(-0.3 * 1)
    outs1 = (_Out(0, C_WIDTH, BF16, "q", q_scale, latent_only=True, transposed=True),
             _Out(C_WIDTH, C_WIDTH, BF16, "k"), _Out(2 * C_WIDTH, C_WIDTH, BF16, transposed=True),
             _Out(3 * C_WIDTH, C_WIDTH, F32))
    mods1 = mod_table(1)
    qt, k, vt, g = _inproj(
        tok, tok, mods1, norm_w[1], od_w_in[0].astype(BF16), cos, sin_signed, tile_norm(od_q_norm[0]),
        tile_norm(od_k_norm[0]), seg_ones2, outs1)
    o = _diff_attn(od_lambda[0], qt, k, vt, od_head_norm[0], lambda_init)
    return _outproj_odd(o, g, tok, mods1, od_w_out[0].astype(BF16))
```

```python
import functools
import math
from typing import NamedTuple

import jax
import jax.numpy as jnp
from jax import lax
from jax.experimental import pallas as pl
from jax.experimental.pallas import tpu as pltpu

D_MODEL = 1024
DEPTH = 2
GRID_W = 64
CTX_LEN = 256
HEAD_DIM = 64
ROPE_BASE = 10000.0
NORM_EPS = 1e-6
BLOCK = 128

A_HEADS = 8
A_KV_HEADS = 2
A_GROUP = A_HEADS // A_KV_HEADS
A_WIDTH = A_HEADS * HEAD_DIM
A_KV_WIDTH = A_KV_HEADS * HEAD_DIM

B_INNER = D_MODEL
B_HEAD_DIM = 64
B_HEADS = B_INNER // B_HEAD_DIM
B_GROUPS = 2
B_HPG = B_HEADS // B_GROUPS
B_STATE = 128
B_CONV = 5
B_CHUNK = 128
B_XBC = B_INNER + 2 * B_GROUPS * B_STATE

C_HEADS = D_MODEL // (2 * HEAD_DIM)
C_VDIM = 2 * HEAD_DIM
C_WIDTH = C_HEADS * C_VDIM

LANES = 128
SUBLANES = 8
ROW_TILE = 256
ATTN_Q_TILE = 2048
ATTN_KEY_TILE = 768
ATTN_COL_BLOCK = 256
VMEM_LIMIT = 56 * 1024 * 1024

F32 = jnp.float32
BF16 = jnp.bfloat16
NEG_INF = float("-inf")
LOG2E = math.log2(math.e)


def _silu(x):
    return x * (1.0 / (1.0 + jnp.exp(-x)))


def _dot(a, b):
    return jnp.dot(a, b, preferred_element_type=F32)


def _dot_nt(a, b):
    return lax.dot_general(a, b, (((1,), (1,)), ((), ())), preferred_element_type=F32)


def _split2(x):
    hi = x.astype(BF16)
    lo = (x - hi.astype(F32)).astype(BF16)
    return hi, lo


def _split3(x):
    hi = x.astype(BF16)
    r = x - hi.astype(F32)
    mid = r.astype(BF16)
    lo = (r - mid.astype(F32)).astype(BF16)
    return hi, mid, lo


def _params(sem):
    return pltpu.CompilerParams(dimension_semantics=sem, vmem_limit_bytes=VMEM_LIMIT)


def _mod_kernel(cc_ref, w_ref, b_ref, o_ref):
    s = _silu(cc_ref[...])
    o_ref[...] = jnp.dot(s, w_ref[...], precision=lax.Precision.HIGHEST,
                         preferred_element_type=F32) + b_ref[...]


def _modulation(cc, mod_w, mod_b):
    depth = mod_w.shape[0]
    n_tiles = 3
    return pl.pallas_call(
        _mod_kernel,
        grid=(depth, n_tiles),
        in_specs=[
            pl.BlockSpec((SUBLANES, D_MODEL), lambda l, n: (0, 0)),
            pl.BlockSpec((None, D_MODEL, D_MODEL), lambda l, n: (l, 0, n)),
            pl.BlockSpec((None, 1, D_MODEL), lambda l, n: (l, 0, n)),
        ],
        out_specs=pl.BlockSpec((None, SUBLANES, D_MODEL), lambda l, n: (l, 0, n)),
        out_shape=jax.ShapeDtypeStruct((depth, SUBLANES, 3 * D_MODEL), F32),
        compiler_params=_params(("arbitrary", "arbitrary")),
        name="modulation",
    )(cc, mod_w, mod_b.reshape(depth, 1, 3 * D_MODEL))


class _Out(NamedTuple):
    start: int
    width: int
    dtype: object
    kind: str = "plain"
    scale: float = 1.0
    latent_only: bool = False
    transposed: bool = False


def _head_norm_rope(a, nw, cos, sin_signed, seg_ones2, first_quarter):
    hi, lo = _split2(a * a)
    ssq = _dot(jnp.concatenate([hi, lo], axis=1), seg_ones2)
    n = a * lax.rsqrt(ssq * (1.0 / HEAD_DIM) + NORM_EPS) * nw
    quarter = HEAD_DIM // 4
    rot = jnp.where(first_quarter, pltpu.roll(n, LANES - quarter, 1), pltpu.roll(n, quarter, 1))
    return n * cos + rot * sin_signed


def _inproj_kernel(ctx_ref, lat_ref, mod_ref, nw_ref, w_ref, cos_ref, sin_ref, qn_ref, kn_ref, ones_ref,
                   *refs, outs, conv):
    if conv is not None:
        xprev_ref, xnext_ref, cw_ref, cbias_ref, dtb_ref = refs[:5]
        refs = refs[5:]
    out_refs = refs[:len(outs)]

    def modulated(x):
        ms = jnp.mean(x * x, axis=-1, keepdims=True)
        y = x * lax.rsqrt(ms + NORM_EPS) * nw_ref[...]
        return (y * (1.0 + mod_ref[1:2, :]) + mod_ref[0:1, :]).astype(BF16)

    x = jnp.where(pl.program_id(1) < CTX_LEN // ROW_TILE, ctx_ref[...], lat_ref[...])
    hb = modulated(x)
    lane = lax.broadcasted_iota(jnp.int32, (x.shape[0], LANES), 1)
    first_quarter = (lane % (HEAD_DIM // 2)) < (HEAD_DIM // 4)
    for out, o_ref in zip(outs, out_refs):
        wide = _dot(hb, w_ref[:, out.start:out.start + out.width])
        if out.kind == "plain" and not out.transposed:
            o_ref[...] = wide.astype(out.dtype)
            continue
        for s in range(out.width // LANES):
            cols = slice(s * LANES, (s + 1) * LANES)
            r = wide[:, cols]
            if out.kind != "plain":
                nw = qn_ref[...] if out.kind == "q" else kn_ref[...]
                r = _head_norm_rope(r, nw, cos_ref[...], sin_ref[...], ones_ref[...], first_quarter) * out.scale
            if out.transposed:
                o_ref[cols, :] = r.astype(out.dtype).T
            else:
                o_ref[:, cols] = r.astype(out.dtype)
    if conv is not None:
        hb_halo = modulated(jnp.concatenate([xprev_ref[...], xnext_ref[...]], axis=0))
        _ssd_inputs(hb, hb_halo, w_ref, cw_ref, cbias_ref, dtb_ref, conv, *refs[len(outs):])


def _ssd_inputs(hb, hb_halo, w_ref, cw_ref, cbias_ref, dtb_ref, conv, xs_ref, c_ref, bt_ref, cbm_ref, dtsp_ref):
    xbc_start, dt_start, n_row = conv
    i = pl.program_id(1)
    ctx_tiles = CTX_LEN // ROW_TILE
    w_xbc = w_ref[:, xbc_start:xbc_start + B_XBC]
    xbc = _dot(hb, w_xbc)
    halo = _dot(hb_halo, w_xbc)
    keep_prev = jnp.where(i <= ctx_tiles, 0.0, 1.0)
    keep_next = jnp.where((i == ctx_tiles - 1) | (i == n_row - 1), 0.0, 1.0)
    ext = jnp.concatenate([halo[:SUBLANES] * keep_prev, xbc, halo[SUBLANES:] * keep_next], axis=0)
    n_ext = ROW_TILE + 2 * SUBLANES
    acc = cbias_ref[...]
    for j in range(B_CONV):
        shifted = ext if j == B_CONV // 2 else pltpu.roll(ext, (B_CONV // 2 - j) % n_ext, 0)
        acc = acc + shifted[SUBLANES:SUBLANES + ROW_TILE, :] * cw_ref[j:j + 1, :]
    act = _silu(acc)
    xs_ref[...] = act[:, :B_INNER]
    for g in range(B_GROUPS):
        b_g = act[:, B_INNER + g * B_STATE:B_INNER + (g + 1) * B_STATE]
        c_g = act[:, B_INNER + (B_GROUPS + g) * B_STATE:B_INNER + (B_GROUPS + g + 1) * B_STATE].astype(BF16)
        c_ref[:, g * B_STATE:(g + 1) * B_STATE] = c_g
        for ch in range(ROW_TILE // B_CHUNK):
            rows = slice(ch * B_CHUNK, (ch + 1) * B_CHUNK)
            bt_ref[ch, g * B_STATE:(g + 1) * B_STATE, :] = b_g[rows].T.astype(BF16)
            cbm_ref[ch, g] = _dot_nt(c_g[rows], b_g[rows].astype(BF16))
    raw = _dot(hb, w_ref[:, dt_start:dt_start + LANES]) + dtb_ref[...]
    dtsp_ref[...] = jnp.maximum(raw, 0.0) + jnp.log(1.0 + jnp.exp(-jnp.abs(raw)))


def _inproj(ctx_src, lat_src, mods, norm_w, w, cos, sin_signed, q_norm, k_norm, seg_ones2, outs, conv=None):
    bsz = lat_src.shape[0]
    ctx_tiles = CTX_LEN // ROW_TILE
    split = ctx_src is not lat_src
    t_len = CTX_LEN + lat_src.shape[1] if split else lat_src.shape[1]
    n_row = t_len // ROW_TILE
    lat_row = (lambda i: jnp.maximum(i - ctx_tiles, 0)) if split else (lambda i: i)
    const = lambda shape: pl.BlockSpec(shape, lambda b, i: (0,) * len(shape))

    def out_spec(out):
        r = (lambda i: jnp.maximum(i - ctx_tiles, 0)) if out.latent_only else (lambda i: i)
        if out.transposed:
            return pl.BlockSpec((None, out.width, ROW_TILE), lambda b, i: (b, 0, r(i)))
        return pl.BlockSpec((None, ROW_TILE, out.width), lambda b, i: (b, r(i), 0))

    def out_shape(out):
        rows = t_len - CTX_LEN if out.latent_only else t_len
        return jax.ShapeDtypeStruct((bsz, out.width, rows) if out.transposed else (bsz, rows, out.width), out.dtype)

    in_specs = [
        pl.BlockSpec((None, ROW_TILE, D_MODEL), lambda b, i: (b, 0, 0)),
        pl.BlockSpec((None, ROW_TILE, D_MODEL), lambda b, i: (b, lat_row(i), 0)),
        pl.BlockSpec((None, None, SUBLANES, D_MODEL), lambda b, i: (b, jnp.minimum(i, 1), 0, 0)),
        const((1, D_MODEL)),
        const(w.shape),
        pl.BlockSpec((ROW_TILE, LANES), lambda b, i: (i, 0)),
        pl.BlockSpec((ROW_TILE, LANES), lambda b, i: (i, 0)),
        const((1, LANES)),
        const((1, LANES)),
        const((2 * LANES, LANES)),
    ]
    args = [ctx_src, lat_src, mods, norm_w.reshape(1, D_MODEL), w, cos, sin_signed, q_norm, k_norm, seg_ones2]
    out_specs = [out_spec(o) for o in outs]
    out_shapes = [out_shape(o) for o in outs]
    conv_cfg = None
    if conv is not None:
        assert split
        xbc_start, dt_start, conv_w, conv_b, dt_bias = conv
        conv_cfg = (xbc_start, dt_start, n_row)
        rows8 = ROW_TILE // SUBLANES
        n_rows8 = lat_src.shape[1] // SUBLANES
        chunks = ROW_TILE // B_CHUNK
        n_chunks = t_len // B_CHUNK
        halo = lambda fn: pl.BlockSpec((None, SUBLANES, D_MODEL), fn)
        in_specs += [
            halo(lambda b, i: (b, jnp.maximum((i - ctx_tiles) * rows8 - 1, 0), 0)),
            halo(lambda b, i: (b, jnp.clip((i - ctx_tiles + 1) * rows8, 0, n_rows8 - 1), 0)),
            const((SUBLANES, B_XBC)), const((1, B_XBC)), const((1, LANES)),
        ]
        args += [lat_src, lat_src, conv_w, conv_b, dt_bias]
        row = lambda width: pl.BlockSpec((None, ROW_TILE, width), lambda b, i: (b, i, 0))
        out_specs += [
            row(B_INNER), row(B_GROUPS * B_STATE),
            pl.BlockSpec((None, chunks, B_GROUPS * B_STATE, B_CHUNK), lambda b, i: (b, i, 0, 0)),
            pl.BlockSpec((None, chunks, B_GROUPS, B_CHUNK, B_CHUNK), lambda b, i: (b, i, 0, 0, 0)),
            row(LANES),
        ]
        out_shapes += [
            jax.ShapeDtypeStruct((bsz, t_len, B_INNER), F32),
            jax.ShapeDtypeStruct((bsz, t_len, B_GROUPS * B_STATE), BF16),
            jax.ShapeDtypeStruct((bsz, n_chunks, B_GROUPS * B_STATE, B_CHUNK), BF16),
            jax.ShapeDtypeStruct((bsz, n_chunks, B_GROUPS, B_CHUNK, B_CHUNK), F32),
            jax.ShapeDtypeStruct((bsz, t_len, LANES), F32),
        ]
    return pl.pallas_call(
        functools.partial(_inproj_kernel, outs=outs, conv=conv_cfg),
        grid=(bsz, n_row),
        in_specs=in_specs,
        out_specs=out_specs,
        out_shape=out_shapes,
        compiler_params=_params(("parallel", "arbitrary")),
        name="inproj",
    )(*args)


def _win_attn_kernel(sink_ref, q_ref, kp_ref, kc_ref, kn_ref, kx_ref, vp_ref, vc_ref, vn_ref, vx_ref,
                     bias_ref, ga_ref, o_ref):
    kk = jnp.concatenate([kp_ref[...], kc_ref[...], kn_ref[...], kx_ref[...]], axis=0)
    vv = jnp.concatenate([vp_ref[...], vc_ref[...], vn_ref[...], vx_ref[...]], axis=0)
    n_keys = kk.shape[0]
    v1 = jnp.concatenate([vv, jnp.ones((n_keys, LANES), BF16)], axis=1)
    rows = A_GROUP * BLOCK
    bias = bias_ref[...]
    lane = lax.broadcasted_iota(jnp.int32, (BLOCK, LANES), 1)
    head_of_lane = lane // HEAD_DIM
    g_of_row = lax.broadcasted_iota(jnp.int32, (rows, 1), 0) // BLOCK
    outs = []
    for hh in range(A_KV_HEADS):
        qs = jnp.concatenate(
            [jnp.where(head_of_lane == hh, q_ref[:, g * LANES:(g + 1) * LANES], jnp.zeros((BLOCK, LANES), BF16))
             for g in range(A_GROUP)], axis=0)
        s = _dot_nt(qs, kk).reshape(A_GROUP, BLOCK, n_keys) + bias[None]
        s = s.reshape(rows, n_keys)
        sink = jnp.zeros((rows, 1), F32)
        for g in range(A_GROUP):
            sink = jnp.where(g_of_row == g, sink_ref[hh * A_GROUP + g] * LOG2E, sink)
        m = jnp.maximum(jnp.max(s, axis=-1, keepdims=True), sink)
        e = jnp.exp2(s - m).astype(BF16)
        ov = _dot(e, v1)
        den = jnp.exp2(sink - m) + ov[:, LANES:]
        outs.append(ov[:, :LANES] * (1.0 / den))
    for g in range(A_GROUP):
        cols = slice(g * LANES, (g + 1) * LANES)
        att = jnp.where(head_of_lane == 0, outs[0][g * BLOCK:(g + 1) * BLOCK], outs[1][g * BLOCK:(g + 1) * BLOCK])
        o_ref[:, cols] = (att * _silu(ga_ref[:, cols])).astype(BF16)


def _win_bias():
    t = jnp.arange(BLOCK)[:, None]
    c = jnp.arange(3 * BLOCK + CTX_LEN)[None, :]
    piece = c // BLOCK
    offset = c - t - BLOCK
    window = (offset >= -BLOCK) & (offset <= BLOCK)
    ctx_keys = jnp.broadcast_to(piece >= 3, window.shape)
    kinds = [
        ctx_keys,
        ctx_keys | (window & (piece >= 1) & (piece < 3)),
        ctx_keys | (window & (piece < 3)),
        ctx_keys | (window & (piece < 2)),
    ]
    return jnp.where(jnp.stack(kinds), 0.0, NEG_INF).astype(F32)


def _win_attn(sink, q, k, v, ga):
    bsz, t_len, _ = q.shape
    n_blocks = t_len // BLOCK
    n_ctx_blocks = CTX_LEN // BLOCK
    assert n_blocks - n_ctx_blocks >= 2
    kv = lambda fn: pl.BlockSpec((None, BLOCK, A_KV_WIDTH), fn)
    prev = lambda b, i: (b, jnp.maximum(i - 1, 0), 0)
    cur = lambda b, i: (b, i, 0)
    nxt = lambda b, i: (b, jnp.minimum(i + 1, n_blocks - 1), 0)
    ctx = pl.BlockSpec((None, CTX_LEN, A_KV_WIDTH), lambda b, i: (b, 0, 0))

    def kind(b, i):
        latent = jnp.where(i == n_ctx_blocks, 1, jnp.where(i == n_blocks - 1, 3, 2))
        return (jnp.where(i < n_ctx_blocks, 0, latent), 0, 0)

    return pl.pallas_call(
        _win_attn_kernel,
        grid=(bsz, n_blocks),
        in_specs=[
            pl.BlockSpec(memory_space=pltpu.SMEM),
            pl.BlockSpec((None, BLOCK, A_WIDTH), cur),
            kv(prev), kv(cur), kv(nxt), ctx,
            kv(prev), kv(cur), kv(nxt), ctx,
            pl.BlockSpec((None, BLOCK, 3 * BLOCK + CTX_LEN), kind),
            pl.BlockSpec((None, BLOCK, A_WIDTH), cur),
        ],
        out_specs=pl.BlockSpec((None, BLOCK, A_WIDTH), cur),
        out_shape=jax.ShapeDtypeStruct((bsz, t_len, A_WIDTH), BF16),
        compiler_params=_params(("parallel", "parallel")),
        name="window_attention",
    )(sink, q, k, k, k, k, v, v, v, v, _win_bias(), ga)


def _ssd_direction(d, xs_ref, c_ref, bt_ref, cbm_ref, dt_ref, alog_ref, expand_ref, state_ref):
    backward = d == 1
    xs = xs_ref[...]
    lane = lax.broadcasted_iota(jnp.int32, (B_CHUNK, LANES), 1)
    row = lax.broadcasted_iota(jnp.int32, (B_CHUNK, LANES), 0)
    dir_lanes = (lane >= d * B_HEADS) & (lane < (d + 1) * B_HEADS)
    dt = dt_ref[...]
    a = -jnp.exp(alog_ref[...])
    dta = jnp.where(dir_lanes, dt * a, 0.0)
    tri_mask = (lane >= row) if backward else (lane <= row)
    tri = jnp.where(tri_mask, 1.0, 0.0).astype(BF16)
    tri_t = jnp.where((row >= lane) if backward else (row <= lane), 1.0, 0.0).astype(BF16)
    la_col = sum(_dot(tri, p) for p in _split3(dta))
    la_row = sum(_dot(p, tri_t) for p in _split3(dta.T))
    last = 0 if backward else B_CHUNK - 1
    ela = jnp.exp(la_col)
    w_end = jnp.exp(la_col[last:last + 1, :] - la_col)
    expand2 = expand_ref[d]

    def widen(v):
        hi, lo = _split2(jnp.where(dir_lanes, v, 0.0))
        return _dot(jnp.concatenate([hi, lo], axis=1), expand2)

    dt_e = widen(dt)
    ela_e = widen(ela)
    wdt_e = widen(w_end * dt)
    xdt = (xs * dt_e).astype(BF16)
    wx = (xs * wdt_e).astype(BF16)
    half_of_lane = lane // B_HEAD_DIM
    slabs = []
    gw = B_HPG * B_HEAD_DIM
    for g in range(B_GROUPS):
        c_g = c_ref[:, g * B_STATE:(g + 1) * B_STATE]
        cb = cbm_ref[g]
        h_in = state_ref[d, g]
        y_inter = _dot(c_g, h_in.astype(BF16)) * ela_e[:, g * gw:(g + 1) * gw]
        for pair in range(B_HPG // 2):
            col0 = g * gw + pair * LANES
            xdt_pair = xdt[:, col0:col0 + LANES]
            acc = y_inter[:, pair * LANES:(pair + 1) * LANES]
            for e in range(2):
                col = d * B_HEADS + g * B_HPG + 2 * pair + e
                seg = la_col[:, col:col + 1] - la_row[col:col + 1, :]
                decay = jnp.exp(jnp.where(tri_mask, seg, NEG_INF))
                m = (cb * decay).astype(BF16)
                rhs = jnp.where(half_of_lane == e, xdt_pair, jnp.zeros_like(xdt_pair))
                acc = acc + _dot(m, rhs)
            slabs.append(acc)
        chunk_decay = ela_e[last:last + 1, g * gw:(g + 1) * gw]
        state_ref[d, g] = h_in * chunk_decay + _dot(bt_ref[g * B_STATE:(g + 1) * B_STATE, :], wx[:, g * gw:(g + 1) * gw])
    return slabs, xs


def _ssd_kernel(xsf_ref, cf_ref, btf_ref, cbf_ref, dtf_ref, xsb_ref, cb_ref, btb_ref, cbb_ref, dtb_ref,
                alog_ref, dskip_ref, expand_ref, yf_ref, yb_ref, state_ref):
    @pl.when(pl.program_id(1) == 0)
    def _():
        state_ref[...] = jnp.zeros_like(state_ref)

    common = (alog_ref, expand_ref, state_ref)
    slabs, xs = _ssd_direction(0, xsf_ref, cf_ref, btf_ref, cbf_ref, dtf_ref, *common)
    for n, slab in enumerate(slabs):
        sl = slice(n * LANES, (n + 1) * LANES)
        yf_ref[:, sl] = slab + dskip_ref[:, sl] * xs[:, sl]
    slabs, _ = _ssd_direction(1, xsb_ref, cb_ref, btb_ref, cbb_ref, dtb_ref, *common)
    for n, slab in enumerate(slabs):
        yb_ref[:, n * LANES:(n + 1) * LANES] = slab


def _ssd(xs, c, bt, cbm, dt, a_log, d_skip_e, expand2):
    bsz, t_len, _ = xs.shape
    n_chunks = t_len // B_CHUNK
    n_ctx_chunks = CTX_LEN // B_CHUNK

    def chunk_b(j):
        return jnp.where(j < n_ctx_chunks, n_ctx_chunks - 1 - j, n_chunks + n_ctx_chunks - 1 - j)

    def specs(chunk_of):
        return [
            pl.BlockSpec((None, B_CHUNK, B_INNER), lambda b, j: (b, chunk_of(j), 0)),
            pl.BlockSpec((None, B_CHUNK, B_GROUPS * B_STATE), lambda b, j: (b, chunk_of(j), 0)),
            pl.BlockSpec((None, None, B_GROUPS * B_STATE, B_CHUNK), lambda b, j: (b, chunk_of(j), 0, 0)),
            pl.BlockSpec((None, None, B_GROUPS, B_CHUNK, B_CHUNK), lambda b, j: (b, chunk_of(j), 0, 0, 0)),
            pl.BlockSpec((None, B_CHUNK, LANES), lambda b, j: (b, chunk_of(j), 0)),
        ]

    const = lambda shape: pl.BlockSpec(shape, lambda b, j: (0,) * len(shape))
    y_spec = lambda chunk_of: pl.BlockSpec((None, B_CHUNK, B_INNER), lambda b, j: (b, chunk_of(j), 0))
    y_shape = jax.ShapeDtypeStruct((bsz, t_len, B_INNER), F32)
    return pl.pallas_call(
        _ssd_kernel,
        grid=(bsz, n_chunks),
        in_specs=specs(lambda j: j) + specs(chunk_b) + [
            const((1, LANES)), const((1, B_INNER)), const((2, 2 * LANES, B_INNER)),
        ],
        out_specs=[y_spec(lambda j: j), y_spec(chunk_b)],
        out_shape=[y_shape, y_shape],
        scratch_shapes=[pltpu.VMEM((2, B_GROUPS, B_STATE, B_HPG * B_HEAD_DIM), F32)],
        compiler_params=_params(("parallel", "arbitrary")),
        name="ssd_scan",
    )(xs, c, bt, cbm, dt, xs, c, bt, cbm, dt, a_log, d_skip_e, expand2)


def _outproj_even_kernel(att_ref, yf_ref, yb_ref, z_ref, ctx_ref, lat_ref, mod_ref, sn_ref, w_ref, o_ref):
    tok = jnp.where(pl.program_id(1) < CTX_LEN // ROW_TILE, ctx_ref[...], lat_ref[...])
    att = att_ref[...]
    y = (yf_ref[...] + yb_ref[...]) * _silu(z_ref[...])
    ms = jnp.mean(y * y, axis=-1, keepdims=True)
    yn = (y * lax.rsqrt(ms + NORM_EPS) * sn_ref[...]).astype(BF16)
    out = _dot(att, w_ref[:A_WIDTH, :]) + _dot(yn, w_ref[A_WIDTH:, :])
    o_ref[...] = tok + mod_ref[2:3, :] * out


def _outproj_even(att, yf, yb, z, ctx, x, mods, ssm_norm, w):
    bsz, t_len, _ = att.shape
    ctx_tiles = CTX_LEN // ROW_TILE
    row = lambda width: pl.BlockSpec((None, ROW_TILE, width), lambda b, i: (b, i, 0))
    const = lambda shape: pl.BlockSpec(shape, lambda b, i: (0,) * len(shape))
    return pl.pallas_call(
        _outproj_even_kernel,
        grid=(bsz, t_len // ROW_TILE),
        in_specs=[
            row(A_WIDTH), row(B_INNER), row(B_INNER), row(B_INNER),
            pl.BlockSpec((None, ROW_TILE, D_MODEL), lambda b, i: (b, 0, 0)),
            pl.BlockSpec((None, ROW_TILE, D_MODEL), lambda b, i: (b, jnp.maximum(i - ctx_tiles, 0), 0)),
            pl.BlockSpec((None, None, SUBLANES, D_MODEL), lambda b, i: (b, jnp.minimum(i, 1), 0, 0)),
            const((1, B_INNER)), const(w.shape),
        ],
        out_specs=row(D_MODEL),
        out_shape=jax.ShapeDtypeStruct((bsz, t_len, D_MODEL), F32),
        compiler_params=_params(("parallel", "parallel")),
        name="outproj_even",
    )(att, yf, yb, z, ctx, x, mods, ssm_norm.reshape(1, B_INNER), w)


ONES_ROWS = 16


def _diff_attn_kernel(lam_ref, qt_ref, k_ref, vt_ref, hn_ref, g_ref, o_ref, s_ref, mx_ref, acc_ref, m_ref,
                      *, key_tile, lambda_init):
    tq = qt_ref.shape[1]
    n_tiles = k_ref.shape[0] // key_tile
    n_cols = tq // ATTN_COL_BLOCK
    row = lax.broadcasted_iota(jnp.int32, (LANES, tq), 0)
    qt = qt_ref[...]
    zero = jnp.zeros_like(qt)
    qz = [jnp.where((row // HEAD_DIM) == c, qt, zero) for c in range(2)]
    ones = jnp.ones((ONES_ROWS, key_tile), BF16)

    def scores(t, slot, n):
        cols = slice(n * ATTN_COL_BLOCK, (n + 1) * ATTN_COL_BLOCK)
        start = pl.multiple_of(t * key_tile, key_tile)
        k = k_ref[pl.ds(start, key_tile), :]
        for c in range(2):
            s = _dot(k, qz[c][:, cols])
            s_ref[slot, c, :, cols] = s
            mx_ref[slot, c, :, cols] = jnp.max(s, axis=0, keepdims=True)

    def consume(t, slot, n):
        cols = slice(n * ATTN_COL_BLOCK, (n + 1) * ATTN_COL_BLOCK)
        start = pl.multiple_of(t * key_tile, key_tile)
        v1 = jnp.concatenate([vt_ref[:, pl.ds(start, key_tile)], ones], axis=0)
        for c in range(2):
            s = s_ref[slot, c, :, cols]
            m_old = m_ref[c, :, cols]
            m_new = jnp.maximum(m_old, mx_ref[slot, c, :, cols])
            alpha = jnp.exp2(m_old - m_new)
            p = jnp.exp2(s - m_new).astype(BF16)
            acc_ref[c, :, cols] = alpha * acc_ref[c, :, cols] + _dot(v1, p)
            m_ref[c, :, cols] = m_new

    m_ref[...] = jnp.full(m_ref.shape, NEG_INF, F32)
    acc_ref[...] = jnp.zeros(acc_ref.shape, F32)
    for n in range(n_cols):
        scores(0, 0, n)

    def body(tt, carry):
        for n in range(n_cols):
            scores(2 * tt + 1, 1, n)
            consume(2 * tt, 0, n)
        for n in range(n_cols):
            scores(2 * tt + 2, 0, n)
            consume(2 * tt + 1, 1, n)
        return carry

    lax.fori_loop(0, (n_tiles - 1) // 2, body, 0)
    for n in range(n_cols):
        consume(n_tiles - 1, 0, n)

    lp = lam_ref[...]
    lam = (jnp.exp(jnp.sum(lp[0:1] * lp[1:2], axis=-1, keepdims=True))
           - jnp.exp(jnp.sum(lp[2:3] * lp[3:4], axis=-1, keepdims=True)) + lambda_init)
    a0 = acc_ref[0]
    a1 = acc_ref[1]
    o = a0[:LANES] * (1.0 / a0[LANES:LANES + 1]) - lam * (a1[:LANES] * (1.0 / a1[LANES:LANES + 1]))
    ms = jnp.mean(o * o, axis=0, keepdims=True)
    o = o * lax.rsqrt(ms + NORM_EPS) * hn_ref[...] * (1.0 - lambda_init)
    o_ref[...] = (o.T * _silu(g_ref[...])).astype(BF16)


def _diff_attn(lam_params, qt, k, vt, head_norm, g, lambda_init):
    bsz, _, seq = qt.shape
    t_len = k.shape[1]
    q_tile = ATTN_Q_TILE
    key_tile = ATTN_KEY_TILE
    assert seq % q_tile == 0 and t_len % key_tile == 0 and (t_len // key_tile) % 2 == 1
    assert q_tile % ATTN_COL_BLOCK == 0
    return pl.pallas_call(
        functools.partial(_diff_attn_kernel, key_tile=key_tile, lambda_init=lambda_init),
        grid=(bsz, C_HEADS, seq // q_tile),
        in_specs=[
            pl.BlockSpec(lam_params.shape, lambda b, h, i: (0, 0)),
            pl.BlockSpec((None, C_VDIM, q_tile), lambda b, h, i: (b, h, i)),
            pl.BlockSpec((None, t_len, C_VDIM), lambda b, h, i: (b, 0, h)),
            pl.BlockSpec((None, C_VDIM, t_len), lambda b, h, i: (b, h, 0)),
            pl.BlockSpec((C_VDIM, 1), lambda b, h, i: (0, 0)),
            pl.BlockSpec((None, q_tile, C_VDIM), lambda b, h, i: (b, i, h)),
        ],
        out_specs=pl.BlockSpec((None, q_tile, C_VDIM), lambda b, h, i: (b, i, h)),
        out_shape=jax.ShapeDtypeStruct((bsz, seq, C_WIDTH), BF16),
        scratch_shapes=[
            pltpu.VMEM((2, 2, key_tile, q_tile), F32),
            pltpu.VMEM((2, 2, 1, q_tile), F32),
            pltpu.VMEM((2, LANES + ONES_ROWS, q_tile), F32),
            pltpu.VMEM((2, 1, q_tile), F32),
        ],
        compiler_params=_params(("parallel", "parallel", "parallel")),
        name="diff_attention",
    )(lam_params, qt, k, vt, head_norm.reshape(C_VDIM, 1), g)


def _outproj_odd_kernel(mix_ref, tok_ref, mod_ref, w_ref, o_ref):
    o_ref[...] = tok_ref[...] + mod_ref[2:3, :] * _dot(mix_ref[...], w_ref[...])


def _outproj_odd(o, tok, mods, w):
    bsz, seq, _ = o.shape
    ctx_tiles = CTX_LEN // ROW_TILE
    lat = lambda width: pl.BlockSpec((None, ROW_TILE, width), lambda b, i: (b, i, 0))
    cat = lambda width: pl.BlockSpec((None, ROW_TILE, width), lambda b, i: (b, i + ctx_tiles, 0))
    return pl.pallas_call(
        _outproj_odd_kernel,
        grid=(bsz, seq // ROW_TILE),
        in_specs=[
            lat(C_WIDTH), cat(D_MODEL),
            pl.BlockSpec((None, None, SUBLANES, D_MODEL), lambda b, i: (b, 1, 0, 0)),
            pl.BlockSpec(w.shape, lambda b, i: (0, 0)),
        ],
        out_specs=lat(D_MODEL),
        out_shape=jax.ShapeDtypeStruct((bsz, seq, D_MODEL), F32),
        compiler_params=_params(("parallel", "parallel")),
        name="outproj_odd",
    )(o, tok, mods, w)


def _rope_tables(seq):
    rows = seq // GRID_W
    row = jnp.repeat(jnp.arange(rows), GRID_W).astype(F32)
    col = jnp.tile(jnp.arange(GRID_W), rows).astype(F32)
    n_freq = HEAD_DIM // 4
    inv_freq = ROPE_BASE ** (-jnp.arange(n_freq, dtype=F32) / n_freq)
    ang_r = row[:, None] * inv_freq
    ang_c = col[:, None] * inv_freq
    ang = jnp.concatenate([ang_r, ang_r, ang_c, ang_c], axis=-1)
    sign = jnp.tile(jnp.concatenate([-jnp.ones((n_freq,), F32), jnp.ones((n_freq,), F32)]), 2)
    cos = jnp.concatenate([jnp.ones((CTX_LEN, HEAD_DIM), F32), jnp.cos(ang)], axis=0)
    sin = jnp.concatenate([jnp.zeros((CTX_LEN, HEAD_DIM), F32), jnp.sin(ang) * sign], axis=0)
    reps = LANES // HEAD_DIM
    return jnp.tile(cos, (1, reps)), jnp.tile(sin, (1, reps))


def _pad_lanes(v):
    return jnp.pad(v, ((0, 0), (0, LANES - v.shape[1])))


def kernel(x, c, ctx, c_ctx, mod_w, mod_b, norm_w, ev_w_in, ev_w_out, ev_q_norm, ev_k_norm, ev_sink,
           ev_conv_w, ev_conv_b, ev_dt_bias, ev_a_log, ev_d_skip, ev_ssm_norm, od_w_in, od_w_out,
           od_q_norm, od_k_norm, od_lambda, od_head_norm):
    assert mod_w.shape[0] == DEPTH == 2
    bsz, seq, _ = x.shape
    assert ctx.shape[1] == CTX_LEN and seq % ROW_TILE == 0 and CTX_LEN == ROW_TILE

    cc = jnp.concatenate([c, c_ctx[None], jnp.zeros((SUBLANES - bsz - 1, D_MODEL), F32)], axis=0)
    mod_all = _modulation(cc, mod_w, mod_b)

    def mod_table(li):
        m = mod_all[li].reshape(SUBLANES, 3, D_MODEL)
        rows = jnp.stack([jnp.broadcast_to(m[bsz], (bsz, 3, D_MODEL)), m[:bsz]], axis=1)
        return jnp.pad(rows, ((0, 0), (0, 0), (0, SUBLANES - 3), (0, 0)))

    cos, sin_signed = _rope_tables(seq)
    seg = jnp.arange(LANES) // HEAD_DIM
    seg_ones = (seg[:, None] == seg[None, :]).astype(BF16)
    seg_ones2 = jnp.concatenate([seg_ones, seg_ones], axis=0)
    tile_norm = lambda w: jnp.tile(w, LANES // HEAD_DIM).reshape(1, LANES)

    perm = jnp.array([(hh * A_GROUP + g) * HEAD_DIM + e
                      for g in range(A_GROUP) for hh in range(A_KV_HEADS) for e in range(HEAD_DIM)])
    w_in = ev_w_in[0]
    o_q, o_k, o_v, o_ga, o_z, o_xbc, o_dt = [int(s) for s in
        [0, A_WIDTH, A_WIDTH + A_KV_WIDTH, A_WIDTH + 2 * A_KV_WIDTH, 2 * A_WIDTH + 2 * A_KV_WIDTH,
         2 * A_WIDTH + 2 * A_KV_WIDTH + B_INNER, 2 * A_WIDTH + 2 * A_KV_WIDTH + B_INNER + B_XBC]]
    w0 = jnp.concatenate([
        w_in[:, o_q:o_k][:, perm], w_in[:, o_k:o_ga], w_in[:, o_ga:o_z][:, perm], w_in[:, o_z:],
        jnp.zeros((D_MODEL, LANES - 2 * B_HEADS), F32)], axis=1).astype(BF16)
    q_scale = HEAD_DIM ** -0.5 * LOG2E
    outs0 = (_Out(o_q, A_WIDTH, BF16, "q", q_scale), _Out(o_k, A_KV_WIDTH, BF16, "k"),
             _Out(o_v, A_KV_WIDTH, BF16), _Out(o_ga, A_WIDTH, F32), _Out(o_z, B_INNER, F32))
    mods0 = mod_table(0)
    conv_w = jnp.pad(ev_conv_w[0], ((0, SUBLANES - B_CONV), (0, 0)))
    q, k, v, ga, z, xs, c_bf, b_t, c_bt, dt_sp = _inproj(
        ctx, x, mods0, norm_w[0], w0, cos, sin_signed, tile_norm(ev_q_norm[0]), tile_norm(ev_k_norm[0]),
        seg_ones2, outs0,
        conv=(o_xbc, o_dt, conv_w, ev_conv_b[0].reshape(1, B_XBC), _pad_lanes(ev_dt_bias[0].reshape(1, 2 * B_HEADS))))
    att = _win_attn(ev_sink[0], q, k, v, ga)
    expand = jnp.stack([(jnp.arange(LANES)[:, None] == d * B_HEADS + jnp.arange(B_INNER)[None, :] // B_HEAD_DIM)
                        for d in range(2)]).astype(BF16)
    expand2 = jnp.concatenate([expand, expand], axis=1)
    yf, yb = _ssd(xs, c_bf, b_t, c_bt, dt_sp, _pad_lanes(ev_a_log[0].reshape(1, 2 * B_HEADS)),
                  jnp.repeat(ev_d_skip[0], B_HEAD_DIM).reshape(1, B_INNER), expand2)
    w_out0 = jnp.concatenate([ev_w_out[0][:A_WIDTH][perm], ev_w_out[0][A_WIDTH:]], axis=0).astype(BF16)
    tok = _outproj_even(att, yf, yb, z, ctx, x, mods0, ev_ssm_norm[0], w_out0)

    lambda_init = 0.8 - 0.6 * math.exp(-0.3 * 1)
    outs1 = (_Out(0, C_WIDTH, BF16, "q", q_scale, latent_only=True, transposed=True),
             _Out(C_WIDTH, C_WIDTH, BF16, "k"), _Out(2 * C_WIDTH, C_WIDTH, BF16, transposed=True),
             _Out(3 * C_WIDTH, C_WIDTH, F32, latent_only=True))
    mods1 = mod_table(1)
    qt, k, vt, g = _inproj(
        tok, tok, mods1, norm_w[1], od_w_in[0].astype(BF16), cos, sin_signed, tile_norm(od_q_norm[0]),
        tile_norm(od_k_norm[0]), seg_ones2, outs1)
    o = _diff_attn(od_lambda[0], qt, k, vt, od_head_norm[0], g, lambda_init)
    return _outproj_odd(o, tok, mods1, od_w_out[0].astype(BF16))
```

```python
import functools
import math
from typing import NamedTuple

import jax
import jax.numpy as jnp
from jax import lax
from jax.experimental import pallas as pl
from jax.experimental.pallas import tpu as pltpu

D_MODEL = 1024
DEPTH = 2
GRID_W = 64
CTX_LEN = 256
HEAD_DIM = 64
ROPE_BASE = 10000.0
NORM_EPS = 1e-6
BLOCK = 128

A_HEADS = 8
A_KV_HEADS = 2
A_GROUP = A_HEADS // A_KV_HEADS
A_WIDTH = A_HEADS * HEAD_DIM
A_KV_WIDTH = A_KV_HEADS * HEAD_DIM

B_INNER = D_MODEL
B_HEAD_DIM = 64
B_HEADS = B_INNER // B_HEAD_DIM
B_GROUPS = 2
B_HPG = B_HEADS // B_GROUPS
B_STATE = 128
B_CONV = 5
B_CHUNK = 128
B_XBC = B_INNER + 2 * B_GROUPS * B_STATE

C_HEADS = D_MODEL // (2 * HEAD_DIM)
C_VDIM = 2 * HEAD_DIM
C_WIDTH = C_HEADS * C_VDIM

LANES = 128
SUBLANES = 8
ROW_TILE = 256
ATTN_Q_TILE = 2048
ATTN_KEY_TILE = 768
ATTN_COL_BLOCK = 256
VMEM_LIMIT = 56 * 1024 * 1024

F32 = jnp.float32
BF16 = jnp.bfloat16
NEG_INF = float("-inf")
LOG2E = math.log2(math.e)


def _silu(x):
    return x * (1.0 / (1.0 + jnp.exp(-x)))


def _dot(a, b):
    return jnp.dot(a, b, preferred_element_type=F32)


def _dot_nt(a, b):
    return lax.dot_general(a, b, (((1,), (1,)), ((), ())), preferred_element_type=F32)


def _split2(x):
    hi = x.astype(BF16)
    lo = (x - hi.astype(F32)).astype(BF16)
    return hi, lo


def _split3(x):
    hi = x.astype(BF16)
    r = x - hi.astype(F32)
    mid = r.astype(BF16)
    lo = (r - mid.astype(F32)).astype(BF16)
    return hi, mid, lo


def _params(sem):
    return pltpu.CompilerParams(dimension_semantics=sem, vmem_limit_bytes=VMEM_LIMIT)


def _mod_kernel(cc_ref, w_ref, b_ref, o_ref):
    s = _silu(cc_ref[...])
    o_ref[...] = jnp.dot(s, w_ref[...], precision=lax.Precision.HIGHEST,
                         preferred_element_type=F32) + b_ref[...]


def _modulation(cc, mod_w, mod_b):
    depth = mod_w.shape[0]
    n_tiles = 3
    return pl.pallas_call(
        _mod_kernel,
        grid=(depth, n_tiles),
        in_specs=[
            pl.BlockSpec((SUBLANES, D_MODEL), lambda l, n: (0, 0)),
            pl.BlockSpec((None, D_MODEL, D_MODEL), lambda l, n: (l, 0, n)),
            pl.BlockSpec((None, 1, D_MODEL), lambda l, n: (l, 0, n)),
        ],
        out_specs=pl.BlockSpec((None, SUBLANES, D_MODEL), lambda l, n: (l, 0, n)),
        out_shape=jax.ShapeDtypeStruct((depth, SUBLANES, 3 * D_MODEL), F32),
        compiler_params=_params(("arbitrary", "arbitrary")),
        name="modulation",
    )(cc, mod_w, mod_b.reshape(depth, 1, 3 * D_MODEL))


class _Out(NamedTuple):
    start: int
    width: int
    dtype: object
    kind: str = "plain"
    scale: float = 1.0
    latent_only: bool = False
    transposed: bool = False


def _head_norm_rope(a, nw, cos, sin_signed, seg_ones2, first_quarter):
    hi, lo = _split2(a * a)
    ssq = _dot(jnp.concatenate([hi, lo], axis=1), seg_ones2)
    n = a * lax.rsqrt(ssq * (1.0 / HEAD_DIM) + NORM_EPS) * nw
    quarter = HEAD_DIM // 4
    rot = jnp.where(first_quarter, pltpu.roll(n, LANES - quarter, 1), pltpu.roll(n, quarter, 1))
    return n * cos + rot * sin_signed


def _inproj_kernel(ctx_ref, lat_ref, mod_ref, nw_ref, w_ref, cos_ref, sin_ref, qn_ref, kn_ref, ones_ref,
                   *refs, outs, conv):
    if conv is not None:
        xprev_ref, xnext_ref, cw_ref, cbias_ref, dtb_ref = refs[:5]
        refs = refs[5:]
    out_refs = refs[:len(outs)]

    def modulated(x):
        ms = jnp.mean(x * x, axis=-1, keepdims=True)
        y = x * lax.rsqrt(ms + NORM_EPS) * nw_ref[...]
        return (y * (1.0 + mod_ref[1:2, :]) + mod_ref[0:1, :]).astype(BF16)

    x = jnp.where(pl.program_id(1) < CTX_LEN // ROW_TILE, ctx_ref[...], lat_ref[...])
    hb = modulated(x)
    lane = lax.broadcasted_iota(jnp.int32, (x.shape[0], LANES), 1)
    first_quarter = (lane % (HEAD_DIM // 2)) < (HEAD_DIM // 4)
    for out, o_ref in zip(outs, out_refs):
        wide = _dot(hb, w_ref[:, out.start:out.start + out.width])
        if out.kind == "plain" and not out.transposed:
            o_ref[...] = wide.astype(out.dtype)
            continue
        for s in range(out.width // LANES):
            cols = slice(s * LANES, (s + 1) * LANES)
            r = wide[:, cols]
            if out.kind != "plain":
                nw = qn_ref[...] if out.kind == "q" else kn_ref[...]
                r = _head_norm_rope(r, nw, cos_ref[...], sin_ref[...], ones_ref[...], first_quarter) * out.scale
            if out.transposed:
                o_ref[cols, :] = r.astype(out.dtype).T
            else:
                o_ref[:, cols] = r.astype(out.dtype)
    if conv is not None:
        hb_halo = modulated(jnp.concatenate([xprev_ref[...], xnext_ref[...]], axis=0))
        _ssd_inputs(hb, hb_halo, w_ref, cw_ref, cbias_ref, dtb_ref, conv, *refs[len(outs):])


def _ssd_inputs(hb, hb_halo, w_ref, cw_ref, cbias_ref, dtb_ref, conv, xs_ref, c_ref, bt_ref, cbm_ref, dtsp_ref):
    xbc_start, dt_start, n_row = conv
    i = pl.program_id(1)
    ctx_tiles = CTX_LEN // ROW_TILE
    w_xbc = w_ref[:, xbc_start:xbc_start + B_XBC]
    xbc = _dot(hb, w_xbc)
    halo = _dot(hb_halo, w_xbc)
    keep_prev = jnp.where(i <= ctx_tiles, 0.0, 1.0)
    keep_next = jnp.where((i == ctx_tiles - 1) | (i == n_row - 1), 0.0, 1.0)
    ext = jnp.concatenate([halo[:SUBLANES] * keep_prev, xbc, halo[SUBLANES:] * keep_next], axis=0)
    n_ext = ROW_TILE + 2 * SUBLANES
    acc = cbias_ref[...]
    for j in range(B_CONV):
        shifted = ext if j == B_CONV // 2 else pltpu.roll(ext, (B_CONV // 2 - j) % n_ext, 0)
        acc = acc + shifted[SUBLANES:SUBLANES + ROW_TILE, :] * cw_ref[j:j + 1, :]
    act = _silu(acc)
    xs_ref[...] = act[:, :B_INNER]
    for g in range(B_GROUPS):
        b_g = act[:, B_INNER + g * B_STATE:B_INNER + (g + 1) * B_STATE]
        c_g = act[:, B_INNER + (B_GROUPS + g) * B_STATE:B_INNER + (B_GROUPS + g + 1) * B_STATE].astype(BF16)
        c_ref[:, g * B_STATE:(g + 1) * B_STATE] = c_g
        for ch in range(ROW_TILE // B_CHUNK):
            rows = slice(ch * B_CHUNK, (ch + 1) * B_CHUNK)
            bt_ref[ch, g * B_STATE:(g + 1) * B_STATE, :] = b_g[rows].T.astype(BF16)
            cbm_ref[ch, g] = _dot_nt(c_g[rows], b_g[rows].astype(BF16))
    raw = _dot(hb, w_ref[:, dt_start:dt_start + LANES]) + dtb_ref[...]
    dtsp_ref[...] = jnp.maximum(raw, 0.0) + jnp.log(1.0 + jnp.exp(-jnp.abs(raw)))


def _inproj(ctx_src, lat_src, mods, norm_w, w, cos, sin_signed, q_norm, k_norm, seg_ones2, outs, conv=None):
    bsz = lat_src.shape[0]
    ctx_tiles = CTX_LEN // ROW_TILE
    split = ctx_src is not lat_src
    t_len = CTX_LEN + lat_src.shape[1] if split else lat_src.shape[1]
    n_row = t_len // ROW_TILE
    lat_row = (lambda i: jnp.maximum(i - ctx_tiles, 0)) if split else (lambda i: i)
    const = lambda shape: pl.BlockSpec(shape, lambda b, i: (0,) * len(shape))

    def out_spec(out):
        r = (lambda i: jnp.maximum(i - ctx_tiles, 0)) if out.latent_only else (lambda i: i)
        if out.transposed:
            return pl.BlockSpec((None, out.width, ROW_TILE), lambda b, i: (b, 0, r(i)))
        return pl.BlockSpec((None, ROW_TILE, out.width), lambda b, i: (b, r(i), 0))

    def out_shape(out):
        rows = t_len - CTX_LEN if out.latent_only else t_len
        return jax.ShapeDtypeStruct((bsz, out.width, rows) if out.transposed else (bsz, rows, out.width), out.dtype)

    in_specs = [
        pl.BlockSpec((None, ROW_TILE, D_MODEL), lambda b, i: (b, 0, 0)),
        pl.BlockSpec((None, ROW_TILE, D_MODEL), lambda b, i: (b, lat_row(i), 0)),
        pl.BlockSpec((None, None, SUBLANES, D_MODEL), lambda b, i: (b, jnp.minimum(i, 1), 0, 0)),
        const((1, D_MODEL)),
        const(w.shape),
        pl.BlockSpec((ROW_TILE, LANES), lambda b, i: (i, 0)),
        pl.BlockSpec((ROW_TILE, LANES), lambda b, i: (i, 0)),
        const((1, LANES)),
        const((1, LANES)),
        const((2 * LANES, LANES)),
    ]
    args = [ctx_src, lat_src, mods, norm_w.reshape(1, D_MODEL), w, cos, sin_signed, q_norm, k_norm, seg_ones2]
    out_specs = [out_spec(o) for o in outs]
    out_shapes = [out_shape(o) for o in outs]
    conv_cfg = None
    if conv is not None:
        assert split
        xbc_start, dt_start, conv_w, conv_b, dt_bias = conv
        conv_cfg = (xbc_start, dt_start, n_row)
        rows8 = ROW_TILE // SUBLANES
        n_rows8 = lat_src.shape[1] // SUBLANES
        chunks = ROW_TILE // B_CHUNK
        n_chunks = t_len // B_CHUNK
        halo = lambda fn: pl.BlockSpec((None, SUBLANES, D_MODEL), fn)
        in_specs += [
            halo(lambda b, i: (b, jnp.maximum((i - ctx_tiles) * rows8 - 1, 0), 0)),
            halo(lambda b, i: (b, jnp.clip((i - ctx_tiles + 1) * rows8, 0, n_rows8 - 1), 0)),
            const((SUBLANES, B_XBC)), const((1, B_XBC)), const((1, LANES)),
        ]
        args += [lat_src, lat_src, conv_w, conv_b, dt_bias]
        row = lambda width: pl.BlockSpec((None, ROW_TILE, width), lambda b, i: (b, i, 0))
        out_specs += [
            row(B_INNER), row(B_GROUPS * B_STATE),
            pl.BlockSpec((None, chunks, B_GROUPS * B_STATE, B_CHUNK), lambda b, i: (b, i, 0, 0)),
            pl.BlockSpec((None, chunks, B_GROUPS, B_CHUNK, B_CHUNK), lambda b, i: (b, i, 0, 0, 0)),
            row(LANES),
        ]
        out_shapes += [
            jax.ShapeDtypeStruct((bsz, t_len, B_INNER), F32),
            jax.ShapeDtypeStruct((bsz, t_len, B_GROUPS * B_STATE), BF16),
            jax.ShapeDtypeStruct((bsz, n_chunks, B_GROUPS * B_STATE, B_CHUNK), BF16),
            jax.ShapeDtypeStruct((bsz, n_chunks, B_GROUPS, B_CHUNK, B_CHUNK), F32),
            jax.ShapeDtypeStruct((bsz, t_len, LANES), F32),
        ]
    return pl.pallas_call(
        functools.partial(_inproj_kernel, outs=outs, conv=conv_cfg),
        grid=(bsz, n_row),
        in_specs=in_specs,
        out_specs=out_specs,
        out_shape=out_shapes,
        compiler_params=_params(("parallel", "arbitrary")),
        name="inproj",
    )(*args)


def _win_attn_kernel(sink_ref, q_ref, kp_ref, kc_ref, kn_ref, kx_ref, vp_ref, vc_ref, vn_ref, vx_ref,
                     bias_ref, ga_ref, o_ref):
    kk = jnp.concatenate([kp_ref[...], kc_ref[...], kn_ref[...], kx_ref[...]], axis=0)
    vv = jnp.concatenate([vp_ref[...], vc_ref[...], vn_ref[...], vx_ref[...]], axis=0)
    n_keys = kk.shape[0]
    v1 = jnp.concatenate([vv, jnp.ones((n_keys, LANES), BF16)], axis=1)
    rows = A_GROUP * BLOCK
    bias = bias_ref[...]
    lane = lax.broadcasted_iota(jnp.int32, (BLOCK, LANES), 1)
    head_of_lane = lane // HEAD_DIM
    g_of_row = lax.broadcasted_iota(jnp.int32, (rows, 1), 0) // BLOCK
    outs = []
    for hh in range(A_KV_HEADS):
        qs = jnp.concatenate(
            [jnp.where(head_of_lane == hh, q_ref[:, g * LANES:(g + 1) * LANES], jnp.zeros((BLOCK, LANES), BF16))
             for g in range(A_GROUP)], axis=0)
        s = _dot_nt(qs, kk).reshape(A_GROUP, BLOCK, n_keys) + bias[None]
        s = s.reshape(rows, n_keys)
        sink = jnp.zeros((rows, 1), F32)
        for g in range(A_GROUP):
            sink = jnp.where(g_of_row == g, sink_ref[hh * A_GROUP + g] * LOG2E, sink)
        m = jnp.maximum(jnp.max(s, axis=-1, keepdims=True), sink)
        e = jnp.exp2(s - m).astype(BF16)
        ov = _dot(e, v1)
        den = jnp.exp2(sink - m) + ov[:, LANES:]
        outs.append(ov[:, :LANES] * (1.0 / den))
    for g in range(A_GROUP):
        cols = slice(g * LANES, (g + 1) * LANES)
        att = jnp.where(head_of_lane == 0, outs[0][g * BLOCK:(g + 1) * BLOCK], outs[1][g * BLOCK:(g + 1) * BLOCK])
        o_ref[:, cols] = (att * _silu(ga_ref[:, cols])).astype(BF16)


def _win_bias():
    t = jnp.arange(BLOCK)[:, None]
    c = jnp.arange(3 * BLOCK + CTX_LEN)[None, :]
    piece = c // BLOCK
    offset = c - t - BLOCK
    window = (offset >= -BLOCK) & (offset <= BLOCK)
    ctx_keys = jnp.broadcast_to(piece >= 3, window.shape)
    kinds = [
        ctx_keys,
        ctx_keys | (window & (piece >= 1) & (piece < 3)),
        ctx_keys | (window & (piece < 3)),
        ctx_keys | (window & (piece < 2)),
    ]
    return jnp.where(jnp.stack(kinds), 0.0, NEG_INF).astype(F32)


def _win_attn(sink, q, k, v, ga):
    bsz, t_len, _ = q.shape
    n_blocks = t_len // BLOCK
    n_ctx_blocks = CTX_LEN // BLOCK
    assert n_blocks - n_ctx_blocks >= 2
    kv = lambda fn: pl.BlockSpec((None, BLOCK, A_KV_WIDTH), fn)
    prev = lambda b, i: (b, jnp.maximum(i - 1, 0), 0)
    cur = lambda b, i: (b, i, 0)
    nxt = lambda b, i: (b, jnp.minimum(i + 1, n_blocks - 1), 0)
    ctx = pl.BlockSpec((None, CTX_LEN, A_KV_WIDTH), lambda b, i: (b, 0, 0))

    def kind(b, i):
        latent = jnp.where(i == n_ctx_blocks, 1, jnp.where(i == n_blocks - 1, 3, 2))
        return (jnp.where(i < n_ctx_blocks, 0, latent), 0, 0)

    return pl.pallas_call(
        _win_attn_kernel,
        grid=(bsz, n_blocks),
        in_specs=[
            pl.BlockSpec(memory_space=pltpu.SMEM),
            pl.BlockSpec((None, BLOCK, A_WIDTH), cur),
            kv(prev), kv(cur), kv(nxt), ctx,
            kv(prev), kv(cur), kv(nxt), ctx,
            pl.BlockSpec((None, BLOCK, 3 * BLOCK + CTX_LEN), kind),
            pl.BlockSpec((None, BLOCK, A_WIDTH), cur),
        ],
        out_specs=pl.BlockSpec((None, BLOCK, A_WIDTH), cur),
        out_shape=jax.ShapeDtypeStruct((bsz, t_len, A_WIDTH), BF16),
        compiler_params=_params(("parallel", "parallel")),
        name="window_attention",
    )(sink, q, k, k, k, k, v, v, v, v, _win_bias(), ga)


def _ssd_direction(d, xs_ref, c_ref, bt_ref, cbm_ref, dt_ref, alog_ref, expand_ref, state_ref):
    backward = d == 1
    xs = xs_ref[...]
    lane = lax.broadcasted_iota(jnp.int32, (B_CHUNK, LANES), 1)
    row = lax.broadcasted_iota(jnp.int32, (B_CHUNK, LANES), 0)
    dir_lanes = (lane >= d * B_HEADS) & (lane < (d + 1) * B_HEADS)
    dt = dt_ref[...]
    a = -jnp.exp(alog_ref[...])
    dta = jnp.where(dir_lanes, dt * a, 0.0)
    tri_mask = (lane >= row) if backward else (lane <= row)
    tri = jnp.where(tri_mask, 1.0, 0.0).astype(BF16)
    tri_t = jnp.where((row >= lane) if backward else (row <= lane), 1.0, 0.0).astype(BF16)
    la_col = sum(_dot(tri, p) for p in _split3(dta))
    la_row = sum(_dot(p, tri_t) for p in _split3(dta.T))
    last = 0 if backward else B_CHUNK - 1
    ela = jnp.exp(la_col)
    w_end = jnp.exp(la_col[last:last + 1, :] - la_col)
    expand2 = expand_ref[d]

    def widen(v):
        hi, lo = _split2(jnp.where(dir_lanes, v, 0.0))
        return _dot(jnp.concatenate([hi, lo], axis=1), expand2)

    dt_e = widen(dt)
    ela_e = widen(ela)
    wdt_e = widen(w_end * dt)
    xdt = (xs * dt_e).astype(BF16)
    wx = (xs * wdt_e).astype(BF16)
    half_of_lane = lane // B_HEAD_DIM
    slabs = []
    gw = B_HPG * B_HEAD_DIM
    for g in range(B_GROUPS):
        c_g = c_ref[:, g * B_STATE:(g + 1) * B_STATE]
        cb = cbm_ref[g]
        h_in = state_ref[d, g]
        y_inter = _dot(c_g, h_in.astype(BF16)) * ela_e[:, g * gw:(g + 1) * gw]
        for pair in range(B_HPG // 2):
            col0 = g * gw + pair * LANES
            xdt_pair = xdt[:, col0:col0 + LANES]
            acc = y_inter[:, pair * LANES:(pair + 1) * LANES]
            for e in range(2):
                col = d * B_HEADS + g * B_HPG + 2 * pair + e
                seg = la_col[:, col:col + 1] - la_row[col:col + 1, :]
                decay = jnp.exp(jnp.where(tri_mask, seg, NEG_INF))
                m = (cb * decay).astype(BF16)
                rhs = jnp.where(half_of_lane == e, xdt_pair, jnp.zeros_like(xdt_pair))
                acc = acc + _dot(m, rhs)
            slabs.append(acc)
        chunk_decay = ela_e[last:last + 1, g * gw:(g + 1) * gw]
        state_ref[d, g] = h_in * chunk_decay + _dot(bt_ref[g * B_STATE:(g + 1) * B_STATE, :], wx[:, g * gw:(g + 1) * gw])
    return slabs, xs


def _ssd_kernel(xsf_ref, cf_ref, btf_ref, cbf_ref, dtf_ref, xsb_ref, cb_ref, btb_ref, cbb_ref, dtb_ref,
                alog_ref, dskip_ref, expand_ref, yf_ref, yb_ref, state_ref):
    @pl.when(pl.program_id(1) == 0)
    def _():
        state_ref[...] = jnp.zeros_like(state_ref)

    common = (alog_ref, expand_ref, state_ref)
    slabs, xs = _ssd_direction(0, xsf_ref, cf_ref, btf_ref, cbf_ref, dtf_ref, *common)
    for n, slab in enumerate(slabs):
        sl = slice(n * LANES, (n + 1) * LANES)
        yf_ref[:, sl] = slab + dskip_ref[:, sl] * xs[:, sl]
    slabs, _ = _ssd_direction(1, xsb_ref, cb_ref, btb_ref, cbb_ref, dtb_ref, *common)
    for n, slab in enumerate(slabs):
        yb_ref[:, n * LANES:(n + 1) * LANES] = slab


def _ssd(xs, c, bt, cbm, dt, a_log, d_skip_e, expand2):
    bsz, t_len, _ = xs.shape
    n_chunks = t_len // B_CHUNK
    n_ctx_chunks = CTX_LEN // B_CHUNK

    def chunk_b(j):
        return jnp.where(j < n_ctx_chunks, n_ctx_chunks - 1 - j, n_chunks + n_ctx_chunks - 1 - j)

    def specs(chunk_of):
        return [
            pl.BlockSpec((None, B_CHUNK, B_INNER), lambda b, j: (b, chunk_of(j), 0)),
            pl.BlockSpec((None, B_CHUNK, B_GROUPS * B_STATE), lambda b, j: (b, chunk_of(j), 0)),
            pl.BlockSpec((None, None, B_GROUPS * B_STATE, B_CHUNK), lambda b, j: (b, chunk_of(j), 0, 0)),
            pl.BlockSpec((None, None, B_GROUPS, B_CHUNK, B_CHUNK), lambda b, j: (b, chunk_of(j), 0, 0, 0)),
            pl.BlockSpec((None, B_CHUNK, LANES), lambda b, j: (b, chunk_of(j), 0)),
        ]

    const = lambda shape: pl.BlockSpec(shape, lambda b, j: (0,) * len(shape))
    y_spec = lambda chunk_of: pl.BlockSpec((None, B_CHUNK, B_INNER), lambda b, j: (b, chunk_of(j), 0))
    y_shape = jax.ShapeDtypeStruct((bsz, t_len, B_INNER), F32)
    return pl.pallas_call(
        _ssd_kernel,
        grid=(bsz, n_chunks),
        in_specs=specs(lambda j: j) + specs(chunk_b) + [
            const((1, LANES)), const((1, B_INNER)), const((2, 2 * LANES, B_INNER)),
        ],
        out_specs=[y_spec(lambda j: j), y_spec(chunk_b)],
        out_shape=[y_shape, y_shape],
        scratch_shapes=[pltpu.VMEM((2, B_GROUPS, B_STATE, B_HPG * B_HEAD_DIM), F32)],
        compiler_params=_params(("parallel", "arbitrary")),
        name="ssd_scan",
    )(xs, c, bt, cbm, dt, xs, c, bt, cbm, dt, a_log, d_skip_e, expand2)


def _outproj_even_kernel(att_ref, yf_ref, yb_ref, z_ref, ctx_ref, lat_ref, mod_ref, sn_ref, w_ref, o_ref):
    tok = jnp.where(pl.program_id(1) < CTX_LEN // ROW_TILE, ctx_ref[...], lat_ref[...])
    att = att_ref[...]
    y = (yf_ref[...] + yb_ref[...]) * _silu(z_ref[...])
    ms = jnp.mean(y * y, axis=-1, keepdims=True)
    yn = (y * lax.rsqrt(ms + NORM_EPS) * sn_ref[...]).astype(BF16)
    out = _dot(att, w_ref[:A_WIDTH, :]) + _dot(yn, w_ref[A_WIDTH:, :])
    o_ref[...] = tok + mod_ref[2:3, :] * out


def _outproj_even(att, yf, yb, z, ctx, x, mods, ssm_norm, w):
    bsz, t_len, _ = att.shape
    ctx_tiles = CTX_LEN // ROW_TILE
    row = lambda width: pl.BlockSpec((None, ROW_TILE, width), lambda b, i: (b, i, 0))
    const = lambda shape: pl.BlockSpec(shape, lambda b, i: (0,) * len(shape))
    return pl.pallas_call(
        _outproj_even_kernel,
        grid=(bsz, t_len // ROW_TILE),
        in_specs=[
            row(A_WIDTH), row(B_INNER), row(B_INNER), row(B_INNER),
            pl.BlockSpec((None, ROW_TILE, D_MODEL), lambda b, i: (b, 0, 0)),
            pl.BlockSpec((None, ROW_TILE, D_MODEL), lambda b, i: (b, jnp.maximum(i - ctx_tiles, 0), 0)),
            pl.BlockSpec((None, None, SUBLANES, D_MODEL), lambda b, i: (b, jnp.minimum(i, 1), 0, 0)),
            const((1, B_INNER)), const(w.shape),
        ],
        out_specs=row(D_MODEL),
        out_shape=jax.ShapeDtypeStruct((bsz, t_len, D_MODEL), F32),
        compiler_params=_params(("parallel", "parallel")),
        name="outproj_even",
    )(att, yf, yb, z, ctx, x, mods, ssm_norm.reshape(1, B_INNER), w)


ONES_ROWS = 16


def _diff_attn_kernel(lam_ref, qt_ref, k_ref, vt_ref, hn_ref, g_ref, o_ref, s_ref, mx_ref, acc_ref, m_ref,
                      *, key_tile, lambda_init):
    tq = qt_ref.shape[1]
    n_tiles = k_ref.shape[0] // key_tile
    n_cols = tq // ATTN_COL_BLOCK
    row = lax.broadcasted_iota(jnp.int32, (LANES, tq), 0)
    qt = qt_ref[...]
    zero = jnp.zeros_like(qt)
    qz = [jnp.where((row // HEAD_DIM) == c, qt, zero) for c in range(2)]
    ones = jnp.ones((ONES_ROWS, key_tile), BF16)

    def scores(t, slot, n):
        cols = slice(n * ATTN_COL_BLOCK, (n + 1) * ATTN_COL_BLOCK)
        start = pl.multiple_of(t * key_tile, key_tile)
        k = k_ref[pl.ds(start, key_tile), :]
        for c in range(2):
            s = _dot(k, qz[c][:, cols])
            s_ref[slot, c, :, cols] = s
            mx_ref[slot, c, :, cols] = jnp.max(s, axis=0, keepdims=True)

    def consume(t, slot, n, first=False):
        cols = slice(n * ATTN_COL_BLOCK, (n + 1) * ATTN_COL_BLOCK)
        start = pl.multiple_of(t * key_tile, key_tile)
        v1 = jnp.concatenate([vt_ref[:, pl.ds(start, key_tile)], ones], axis=0)
        for c in range(2):
            s = s_ref[slot, c, :, cols]
            if first:
                m_new = mx_ref[slot, c, :, cols]
                acc_ref[c, :, cols] = _dot(v1, jnp.exp2(s - m_new).astype(BF16))
            else:
                m_old = m_ref[c, :, cols]
                m_new = jnp.maximum(m_old, mx_ref[slot, c, :, cols])
                alpha = jnp.exp2(m_old - m_new)
                p = jnp.exp2(s - m_new).astype(BF16)
                acc_ref[c, :, cols] = alpha * acc_ref[c, :, cols] + _dot(v1, p)
            m_ref[c, :, cols] = m_new

    for n in range(n_cols):
        scores(0, 0, n)

    def body(tt, carry, first=False):
        for n in range(n_cols):
            scores(2 * tt + 1, 1, n)
            consume(2 * tt, 0, n, first=first)
        for n in range(n_cols):
            scores(2 * tt + 2, 0, n)
            consume(2 * tt + 1, 1, n)
        return carry

    body(0, 0, first=True)
    lax.fori_loop(1, (n_tiles - 1) // 2, body, 0)
    for n in range(n_cols):
        consume(n_tiles - 1, 0, n)

    lp = lam_ref[...]
    lam = (jnp.exp(jnp.sum(lp[0:1] * lp[1:2], axis=-1, keepdims=True))
           - jnp.exp(jnp.sum(lp[2:3] * lp[3:4], axis=-1, keepdims=True)) + lambda_init)
    a0 = acc_ref[0]
    a1 = acc_ref[1]
    o = a0[:LANES] * (1.0 / a0[LANES:LANES + 1]) - lam * (a1[:LANES] * (1.0 / a1[LANES:LANES + 1]))
    ms = jnp.mean(o * o, axis=0, keepdims=True)
    o = o * lax.rsqrt(ms + NORM_EPS) * hn_ref[...] * (1.0 - lambda_init)
    o_ref[...] = (o.T * _silu(g_ref[...])).astype(BF16)


def _diff_attn(lam_params, qt, k, vt, head_norm, g, lambda_init):
    bsz, _, seq = qt.shape
    t_len = k.shape[1]
    q_tile = ATTN_Q_TILE
    key_tile = ATTN_KEY_TILE
    assert seq % q_tile == 0 and t_len % key_tile == 0 and (t_len // key_tile) % 2 == 1
    assert q_tile % ATTN_COL_BLOCK == 0
    return pl.pallas_call(
        functools.partial(_diff_attn_kernel, key_tile=key_tile, lambda_init=lambda_init),
        grid=(bsz, C_HEADS, seq // q_tile),
        in_specs=[
            pl.BlockSpec(lam_params.shape, lambda b, h, i: (0, 0)),
            pl.BlockSpec((None, C_VDIM, q_tile), lambda b, h, i: (b, h, i)),
            pl.BlockSpec((None, t_len, C_VDIM), lambda b, h, i: (b, 0, h)),
            pl.BlockSpec((None, C_VDIM, t_len), lambda b, h, i: (b, h, 0)),
            pl.BlockSpec((C_VDIM, 1), lambda b, h, i: (0, 0)),
            pl.BlockSpec((None, q_tile, C_VDIM), lambda b, h, i: (b, i, h)),
        ],
        out_specs=pl.BlockSpec((None, q_tile, C_VDIM), lambda b, h, i: (b, i, h)),
        out_shape=jax.ShapeDtypeStruct((bsz, seq, C_WIDTH), BF16),
        scratch_shapes=[
            pltpu.VMEM((2, 2, key_tile, q_tile), F32),
            pltpu.VMEM((2, 2, 1, q_tile), F32),
            pltpu.VMEM((2, LANES + ONES_ROWS, q_tile), F32),
            pltpu.VMEM((2, 1, q_tile), F32),
        ],
        compiler_params=_params(("parallel", "parallel", "parallel")),
        name="diff_attention",
    )(lam_params, qt, k, vt, head_norm.reshape(C_VDIM, 1), g)


def _outproj_odd_kernel(mix_ref, tok_ref, mod_ref, w_ref, o_ref):
    o_ref[...] = tok_ref[...] + mod_ref[2:3, :] * _dot(mix_ref[...], w_ref[...])


def _outproj_odd(o, tok, mods, w):
    bsz, seq, _ = o.shape
    ctx_tiles = CTX_LEN // ROW_TILE
    lat = lambda width: pl.BlockSpec((None, ROW_TILE, width), lambda b, i: (b, i, 0))
    cat = lambda width: pl.BlockSpec((None, ROW_TILE, width), lambda b, i: (b, i + ctx_tiles, 0))
    return pl.pallas_call(
        _outproj_odd_kernel,
        grid=(bsz, seq // ROW_TILE),
        in_specs=[
            lat(C_WIDTH), cat(D_MODEL),
            pl.BlockSpec((None, None, SUBLANES, D_MODEL), lambda b, i: (b, 1, 0, 0)),
            pl.BlockSpec(w.shape, lambda b, i: (0, 0)),
        ],
        out_specs=lat(D_MODEL),
        out_shape=jax.ShapeDtypeStruct((bsz, seq, D_MODEL), F32),
        compiler_params=_params(("parallel", "parallel")),
        name="outproj_odd",
    )(o, tok, mods, w)


def _rope_tables(seq):
    rows = seq // GRID_W
    row = jnp.repeat(jnp.arange(rows), GRID_W).astype(F32)
    col = jnp.tile(jnp.arange(GRID_W), rows).astype(F32)
    n_freq = HEAD_DIM // 4
    inv_freq = ROPE_BASE ** (-jnp.arange(n_freq, dtype=F32) / n_freq)
    ang_r = row[:, None] * inv_freq
    ang_c = col[:, None] * inv_freq
    ang = jnp.concatenate([ang_r, ang_r, ang_c, ang_c], axis=-1)
    sign = jnp.tile(jnp.concatenate([-jnp.ones((n_freq,), F32), jnp.ones((n_freq,), F32)]), 2)
    cos = jnp.concatenate([jnp.ones((CTX_LEN, HEAD_DIM), F32), jnp.cos(ang)], axis=0)
    sin = jnp.concatenate([jnp.zeros((CTX_LEN, HEAD_DIM), F32), jnp.sin(ang) * sign], axis=0)
    reps = LANES // HEAD_DIM
    return jnp.tile(cos, (1, reps)), jnp.tile(sin, (1, reps))


def _pad_lanes(v):
    return jnp.pad(v, ((0, 0), (0, LANES - v.shape[1])))


def kernel(x, c, ctx, c_ctx, mod_w, mod_b, norm_w, ev_w_in, ev_w_out, ev_q_norm, ev_k_norm, ev_sink,
           ev_conv_w, ev_conv_b, ev_dt_bias, ev_a_log, ev_d_skip, ev_ssm_norm, od_w_in, od_w_out,
           od_q_norm, od_k_norm, od_lambda, od_head_norm):
    assert mod_w.shape[0] == DEPTH == 2
    bsz, seq, _ = x.shape
    assert ctx.shape[1] == CTX_LEN and seq % ROW_TILE == 0 and CTX_LEN == ROW_TILE

    cc = jnp.concatenate([c, c_ctx[None], jnp.zeros((SUBLANES - bsz - 1, D_MODEL), F32)], axis=0)
    mod_all = _modulation(cc, mod_w, mod_b)

    def mod_table(li):
        m = mod_all[li].reshape(SUBLANES, 3, D_MODEL)
        rows = jnp.stack([jnp.broadcast_to(m[bsz], (bsz, 3, D_MODEL)), m[:bsz]], axis=1)
        return jnp.pad(rows, ((0, 0), (0, 0), (0, SUBLANES - 3), (0, 0)))

    cos, sin_signed = _rope_tables(seq)
    seg = jnp.arange(LANES) // HEAD_DIM
    seg_ones = (seg[:, None] == seg[None, :]).astype(BF16)
    seg_ones2 = jnp.concatenate([seg_ones, seg_ones], axis=0)
    tile_norm = lambda w: jnp.tile(w, LANES // HEAD_DIM).reshape(1, LANES)

    perm = jnp.array([(hh * A_GROUP + g) * HEAD_DIM + e
                      for g in range(A_GROUP) for hh in range(A_KV_HEADS) for e in range(HEAD_DIM)])
    w_in = ev_w_in[0]
    o_q, o_k, o_v, o_ga, o_z, o_xbc, o_dt = [int(s) for s in
        [0, A_WIDTH, A_WIDTH + A_KV_WIDTH, A_WIDTH + 2 * A_KV_WIDTH, 2 * A_WIDTH + 2 * A_KV_WIDTH,
         2 * A_WIDTH + 2 * A_KV_WIDTH + B_INNER, 2 * A_WIDTH + 2 * A_KV_WIDTH + B_INNER + B_XBC]]
    w0 = jnp.concatenate([
        w_in[:, o_q:o_k][:, perm], w_in[:, o_k:o_ga], w_in[:, o_ga:o_z][:, perm], w_in[:, o_z:],
        jnp.zeros((D_MODEL, LANES - 2 * B_HEADS), F32)], axis=1).astype(BF16)
    q_scale = HEAD_DIM ** -0.5 * LOG2E
    outs0 = (_Out(o_q, A_WIDTH, BF16, "q", q_scale), _Out(o_k, A_KV_WIDTH, BF16, "k"),
             _Out(o_v, A_KV_WIDTH, BF16), _Out(o_ga, A_WIDTH, F32), _Out(o_z, B_INNER, F32))
    mods0 = mod_table(0)
    conv_w = jnp.pad(ev_conv_w[0], ((0, SUBLANES - B_CONV), (0, 0)))
    q, k, v, ga, z, xs, c_bf, b_t, c_bt, dt_sp = _inproj(
        ctx, x, mods0, norm_w[0], w0, cos, sin_signed, tile_norm(ev_q_norm[0]), tile_norm(ev_k_norm[0]),
        seg_ones2, outs0,
        conv=(o_xbc, o_dt, conv_w, ev_conv_b[0].reshape(1, B_XBC), _pad_lanes(ev_dt_bias[0].reshape(1, 2 * B_HEADS))))
    att = _win_attn(ev_sink[0], q, k, v, ga)
    expand = jnp.stack([(jnp.arange(LANES)[:, None] == d * B_HEADS + jnp.arange(B_INNER)[None, :] // B_HEAD_DIM)
                        for d in range(2)]).astype(BF16)
    expand2 = jnp.concatenate([expand, expand], axis=1)
    yf, yb = _ssd(xs, c_bf, b_t, c_bt, dt_sp, _pad_lanes(ev_a_log[0].reshape(1, 2 * B_HEADS)),
                  jnp.repeat(ev_d_skip[0], B_HEAD_DIM).reshape(1, B_INNER), expand2)
    w_out0 = jnp.concatenate([ev_w_out[0][:A_WIDTH][perm], ev_w_out[0][A_WIDTH:]], axis=0).astype(BF16)
    tok = _outproj_even(att, yf, yb, z, ctx, x, mods0, ev_ssm_norm[0], w_out0)

    lambda_init = 0.8 - 0.6 * math.exp(-0.3 * 1)
    outs1 = (_Out(0, C_WIDTH, BF16, "q", q_scale, latent_only=True, transposed=True),
             _Out(C_WIDTH, C_WIDTH, BF16, "k"), _Out(2 * C_WIDTH, C_WIDTH, BF16, transposed=True),
             _Out(3 * C_WIDTH, C_WIDTH, F32, latent_only=True))
    mods1 = mod_table(1)
    qt, k, vt, g = _inproj(
        tok, tok, mods1, norm_w[1], od_w_in[0].astype(BF16), cos, sin_signed, tile_norm(od_q_norm[0]),
        tile_norm(od_k_norm[0]), seg_ones2, outs1)
    o = _diff_attn(od_lambda[0], qt, k, vt, od_head_norm[0], g, lambda_init)
    return _outproj_odd(o, tok, mods1, od_w_out[0].astype(BF16))
```
